```python
import math
import jax, jax.numpy as jnp
from jax import lax
import numpy as np

D_MODEL = 1024
BATCH = 4
SEQ = 4096
DEPTH = 2
DEC_BATCH = 32
DEC_SEQ = 8
PAST_LEN = 16384
PAGE_SIZE = 128

MIX_WIDTH = D_MODEL
RW_HEAD_DIM = 64
RW_WIDTH = MIX_WIDTH // 2
RW_HEADS = RW_WIDTH // RW_HEAD_DIM
RW_DECAY_LORA = 64
RW_AAA_LORA = 64
RW_GATE_LORA = 128
RW_LN_EPS = 64e-5
RW_PROJ = 3 * RW_WIDTH + RW_DECAY_LORA + RW_AAA_LORA + RW_GATE_LORA
DA_HEAD_DIM = 64
DA_V_DIM = 2 * DA_HEAD_DIM
DA_WIDTH = MIX_WIDTH - RW_WIDTH
DA_HEADS = DA_WIDTH // DA_V_DIM
DA_QK = DA_HEADS * 2 * DA_HEAD_DIM
DA_PROJ = 2 * DA_QK + DA_WIDTH
IN_PROJ = RW_PROJ + DA_PROJ
ALIBI_MAX_BIAS = 8.0
Q_BLOCK = 128
D_FF = -(-8 * D_MODEL // (3 * 256)) * 256
NORM_EPS = 1e-5
SUBLN_EPS = 1e-5

kernel_name = "hymba_rwkv7_diffattn_step"

F32 = jnp.float32


def rmsnorm(x, g, eps=NORM_EPS):
    xf = x.astype(F32)
    y = xf * lax.rsqrt(jnp.mean(xf * xf, axis=-1, keepdims=True) + eps)
    return (y * g.astype(F32)).astype(x.dtype)


def lambda_init_of(layer):
    return 0.8 - 0.6 * math.exp(-0.3 * layer)


def alibi_slopes():
    return jnp.exp2(-ALIBI_MAX_BIAS / DA_HEADS * jnp.arange(1, DA_HEADS + 1, dtype=F32))


def wkv_scan(S0, r, decay, k, v, a, b):
    def step(S, inp):
        r_t, w_t, k_t, v_t, a_t, b_t = inp
        Sa = jnp.einsum('bhvk,bhk->bhv', S, a_t)
        S = S * w_t[:, :, None, :] + Sa[..., None] * b_t[:, :, None, :] + v_t[..., None] * k_t[:, :, None, :]
        y = jnp.einsum('bhvk,bhk->bhv', S, r_t)
        return S, y
    xs = tuple(jnp.swapaxes(t, 0, 1) for t in (r, decay, k, v, a, b))
    S, ys = lax.scan(step, S0, xs)
    return S, jnp.swapaxes(ys, 0, 1)


def rwkv7_mixer(p_rw, prev_row, S0, mu, w0, w2, a0, a2, g2, k_k, k_a, r_k, lnx_w, lnx_b):
    B, T, _ = p_rw.shape
    prev = jnp.concatenate([prev_row[:, None].astype(p_rw.dtype), p_rw[:, :-1]], axis=1)
    xs = p_rw + (prev - p_rw) * mu
    c1, c2, c3 = RW_WIDTH, 2 * RW_WIDTH, 3 * RW_WIDTH
    c4 = c3 + RW_DECAY_LORA
    c5 = c4 + RW_AAA_LORA
    r, k, v, wd, ad, gd = jnp.split(xs, [c1, c2, c3, c4, c5], axis=-1)
    w = -jax.nn.softplus(-(w0 + jnp.tanh(wd) @ w2).astype(F32)) - 0.5
    decay = jnp.exp(-jnp.exp(w))
    a = jax.nn.sigmoid((a0 + ad @ a2).astype(F32))
    g = jax.nn.sigmoid(gd) @ g2
    heads = lambda t: t.reshape(B, T, RW_HEADS, RW_HEAD_DIM)
    kk = heads((k * k_k).astype(F32))
    kk = kk / jnp.maximum(jnp.sqrt(jnp.sum(kk * kk, axis=-1, keepdims=True)), 1e-12)
    k_h = heads(k.astype(F32) * (1.0 + (a - 1.0) * k_a.astype(F32)))
    a_h = heads(a)
    r_h = heads(r.astype(F32))
    v_h = heads(v.astype(F32))
    S, y = wkv_scan(S0.astype(F32), r_h, heads(decay), k_h, v_h, -kk, kk * a_h)
    mean = jnp.mean(y, axis=-1, keepdims=True)
    var = jnp.mean(jnp.square(y - mean), axis=-1, keepdims=True)
    y = ((y - mean) * lax.rsqrt(var + RW_LN_EPS)).reshape(B, T, RW_WIDTH)
    y = y * lnx_w.astype(F32) + lnx_b.astype(F32)
    bonus = jnp.sum(r_h * k_h * r_k.astype(F32), axis=-1, keepdims=True) * v_h
    y = (y + bonus.reshape(B, T, RW_WIDTH)) * g.astype(F32)
    return y.astype(p_rw.dtype), S, p_rw[:, -1]


def diff_attn_prompt(q, k, v, lam, slopes):
    B, T = q.shape[:2]
    nb = T // Q_BLOCK
    qb = jnp.moveaxis(q.reshape(B, nb, Q_BLOCK, DA_HEADS, 2, DA_HEAD_DIM), 1, 0)
    pos_k = jnp.arange(T)
    scale = DA_HEAD_DIM ** -0.5

    def block(args):
        qi, i = args
        pos_q = i * Q_BLOCK + jnp.arange(Q_BLOCK)
        dist = pos_q[:, None] - pos_k[None, :]
        s = jnp.einsum('bqhcd,bkhcd->bhcqk', qi, k).astype(F32) * scale
        s = s - slopes[None, :, None, None, None] * dist.astype(F32)
        s = jnp.where(dist >= 0, s, -jnp.inf)
        p = jax.nn.softmax(s, axis=-1)
        attn = p[:, :, 0] - lam * p[:, :, 1]
        return jnp.einsum('bhqk,bkhe->bqhe', attn.astype(v.dtype), v)

    out = lax.map(block, (qb, jnp.arange(nb)))
    return jnp.moveaxis(out, 0, 1).reshape(B, T, DA_HEADS, DA_V_DIM)


def diff_attn_sample(q, k, v, k_past, v_past, lam, slopes):
    Q = q.shape[1]
    P = k_past.shape[1]
    scale = DA_HEAD_DIM ** -0.5
    pos_q = P + jnp.arange(Q)
    dist_past = (pos_q[:, None] - jnp.arange(P)[None, :]).astype(F32)
    dist_new = jnp.arange(Q)[:, None] - jnp.arange(Q)[None, :]
    sl = slopes[None, :, None, None, None]
    s_past = jnp.einsum('bqhcd,bkhcd->bhcqk', q, k_past).astype(F32) * scale - sl * dist_past
    s_new = jnp.einsum('bqhcd,bkhcd->bhcqk', q, k).astype(F32) * scale - sl * dist_new.astype(F32)
    s_new = jnp.where(dist_new >= 0, s_new, -jnp.inf)
    p = jax.nn.softmax(jnp.concatenate([s_past, s_new], axis=-1), axis=-1)
    attn = (p[:, :, 0] - lam * p[:, :, 1]).astype(v.dtype)
    return (jnp.einsum('bhqk,bkhe->bqhe', attn[..., :P], v_past)
            + jnp.einsum('bhqk,bkhe->bqhe', attn[..., P:], v))


def diff_head_norm(o, subln_w, lam_init):
    of = o.astype(F32)
    of = of * lax.rsqrt(jnp.mean(of * of, axis=-1, keepdims=True) + SUBLN_EPS) * subln_w.astype(F32)
    return of * (1.0 - lam_init)


def decoder_layer(x, prev_row, S0, attend, g1, w_in_l, rw, lam_init, subln_w, w_out_l, g2, w1, w3, w2):
    B, T, _ = x.shape
    h = rmsnorm(x, g1)
    p = h @ w_in_l
    p_rw, q, k, v = jnp.split(p, [RW_PROJ, RW_PROJ + DA_QK, RW_PROJ + 2 * DA_QK], axis=-1)
    o_rw, S, last_row = rwkv7_mixer(p_rw, prev_row, S0, *rw)
    q = q.reshape(B, T, DA_HEADS, 2, DA_HEAD_DIM)
    k = k.reshape(B, T, DA_HEADS, 2, DA_HEAD_DIM)
    v = v.reshape(B, T, DA_HEADS, DA_V_DIM)
    o_da = diff_head_norm(attend(q, k, v), subln_w, lam_init)
    mix = jnp.concatenate([o_rw, o_da.reshape(B, T, DA_WIDTH).astype(x.dtype)], axis=-1)
    x = x + mix @ w_out_l
    hf = rmsnorm(x, g2)
    x = x + (jax.nn.silu(hf @ w1) * (hf @ w3)) @ w2
    return x, k.reshape(B, T, DA_HEADS, 2 * DA_HEAD_DIM), v, S, last_row


def setup_inputs(seed: int = 0) -> dict:
    key = jax.random.key(seed)
    ks = iter(jax.random.split(key, 40))
    nrm = lambda shape, s: jax.random.normal(next(ks), shape, F32) * s
    n_pages = PAST_LEN // PAGE_SIZE
    n_used = DEC_BATCH * n_pages
    n_pool = n_used + n_used // 4 + 1
    d = {}
    d['x_prompt'] = nrm((BATCH, SEQ, D_MODEL), 1.0)
    d['x_sample'] = nrm((DEC_BATCH, DEC_SEQ, D_MODEL), 1.0)
    d['cache_k'] = nrm((DEPTH, n_pool, PAGE_SIZE, DA_HEADS, 2 * DA_HEAD_DIM), 1.0)
    d['cache_v'] = nrm((DEPTH, n_pool, PAGE_SIZE, DA_HEADS, DA_V_DIM), 1.0)
    d['state_wkv'] = nrm((DEPTH, DEC_BATCH, RW_HEADS, RW_HEAD_DIM, RW_HEAD_DIM), 0.3)
    d['state_shift'] = nrm((DEPTH, DEC_BATCH, RW_PROJ), 1.0)
    perm = jax.random.permutation(next(ks), n_pool)[:n_used]
    d['page_table'] = perm.reshape(DEC_BATCH, n_pages).astype(jnp.int32)
    d['norm1_g'] = 1.0 + nrm((DEPTH, D_MODEL), 0.02)
    d['w_in'] = nrm((DEPTH, D_MODEL, IN_PROJ), D_MODEL ** -0.5)
    d['rw_mu'] = jax.random.uniform(next(ks), (DEPTH, RW_PROJ), F32)
    d['rw_w0'] = jax.random.uniform(next(ks), (DEPTH, RW_WIDTH), F32, -6.0, 1.0)
    d['rw_w2'] = nrm((DEPTH, RW_DECAY_LORA, RW_WIDTH), 0.1)
    d['rw_a0'] = nrm((DEPTH, RW_WIDTH), 0.1)
    d['rw_a2'] = nrm((DEPTH, RW_AAA_LORA, RW_WIDTH), 0.1)
    d['rw_g2'] = nrm((DEPTH, RW_GATE_LORA, RW_WIDTH), RW_GATE_LORA ** -0.5)
    d['rw_k_k'] = 0.85 + nrm((DEPTH, RW_WIDTH), 0.05)
    d['rw_k_a'] = 1.0 + nrm((DEPTH, RW_WIDTH), 0.05)
    d['rw_r_k'] = nrm((DEPTH, RW_HEADS, RW_HEAD_DIM), 0.1)
    d['rw_lnx_w'] = 1.0 + nrm((DEPTH, RW_WIDTH), 0.02)
    d['rw_lnx_b'] = nrm((DEPTH, RW_WIDTH), 0.02)
    d['da_lam_q1'] = nrm((DEPTH, DA_HEAD_DIM), 0.1)
    d['da_lam_k1'] = nrm((DEPTH, DA_HEAD_DIM), 0.1)
    d['da_lam_q2'] = nrm((DEPTH, DA_HEAD_DIM), 0.1)
    d['da_lam_k2'] = nrm((DEPTH, DA_HEAD_DIM), 0.1)
    d['da_subln_w'] = 1.0 + nrm((DEPTH, DA_V_DIM), 0.02)
    d['w_out'] = nrm((DEPTH, MIX_WIDTH, D_MODEL), MIX_WIDTH ** -0.5)
    d['norm2_g'] = 1.0 + nrm((DEPTH, D_MODEL), 0.02)
    d['ffn_w1'] = nrm((DEPTH, D_MODEL, D_FF), D_MODEL ** -0.5)
    d['ffn_w3'] = nrm((DEPTH, D_MODEL, D_FF), D_MODEL ** -0.5)
    d['ffn_w2'] = nrm((DEPTH, D_FF, D_MODEL), D_FF ** -0.5)
    d['final_g'] = 1.0 + nrm((D_MODEL,), 0.02)
    return d


def reference(x_prompt, x_sample, cache_k, cache_v, state_wkv, state_shift, page_table,
              norm1_g, w_in, rw_mu, rw_w0, rw_w2, rw_a0, rw_a2, rw_g2, rw_k_k, rw_k_a, rw_r_k,
              rw_lnx_w, rw_lnx_b, da_lam_q1, da_lam_k1, da_lam_q2, da_lam_k2, da_subln_w,
              w_out, norm2_g, ffn_w1, ffn_w3, ffn_w2, final_g):
    slopes = alibi_slopes()
    B = x_prompt.shape[0]
    DB = x_sample.shape[0]
    n_pages = page_table.shape[1]
    past = n_pages * cache_k.shape[2]
    xp, xs = x_prompt, x_sample
    kp_l, vp_l, Sp_l, shp_l = [], [], [], []
    ks_l, vs_l, Ss_l, shs_l = [], [], [], []
    for l in range(DEPTH):
        lam_init = lambda_init_of(l)
        lam = (jnp.exp(jnp.sum(da_lam_q1[l].astype(F32) * da_lam_k1[l].astype(F32)))
               - jnp.exp(jnp.sum(da_lam_q2[l].astype(F32) * da_lam_k2[l].astype(F32))) + lam_init)
        rw = (rw_mu[l], rw_w0[l], rw_w2[l], rw_a0[l], rw_a2[l], rw_g2[l], rw_k_k[l], rw_k_a[l],
              rw_r_k[l], rw_lnx_w[l], rw_lnx_b[l])
        shared = (rw, lam_init, da_subln_w[l], w_out[l], norm2_g[l], ffn_w1[l], ffn_w3[l], ffn_w2[l])
        prev0 = jnp.zeros((B, RW_PROJ), xp.dtype)
        S0 = jnp.zeros((B, RW_HEADS, RW_HEAD_DIM, RW_HEAD_DIM), F32)
        attend_p = lambda q, k, v: diff_attn_prompt(q, k, v, lam, slopes)
        xp, k_new, v_new, S_new, sh_new = decoder_layer(xp, prev0, S0, attend_p, norm1_g[l], w_in[l], *shared)
        kp_l.append(k_new); vp_l.append(v_new); Sp_l.append(S_new); shp_l.append(sh_new)
        k_past = cache_k[l, page_table].reshape(DB, past, DA_HEADS, 2, DA_HEAD_DIM)
        v_past = cache_v[l, page_table].reshape(DB, past, DA_HEADS, DA_V_DIM)
        attend_s = lambda q, k, v: diff_attn_sample(q, k, v, k_past, v_past, lam, slopes)
        xs, k_new, v_new, S_new, sh_new = decoder_layer(xs, state_shift[l], state_wkv[l], attend_s,
                                                        norm1_g[l], w_in[l], *shared)
        ks_l.append(k_new); vs_l.append(v_new); Ss_l.append(S_new); shs_l.append(sh_new)
    y_prompt = rmsnorm(xp, final_g)
    y_sample = rmsnorm(xs, final_g)
    return (y_prompt, y_sample,
            jnp.stack(kp_l), jnp.stack(vp_l), jnp.stack(Sp_l), jnp.stack(shp_l),
            jnp.stack(ks_l), jnp.stack(vs_l), jnp.stack(Ss_l), jnp.stack(shs_l))
```

```python
import functools
import math

import jax
import jax.numpy as jnp
import numpy as np
from jax import lax
from jax.experimental import pallas as pl
from jax.experimental.pallas import tpu as pltpu

F32 = jnp.float32
BF16 = jnp.bfloat16
HIGHEST = lax.Precision.HIGHEST

D_MODEL = 1024
RW_HEAD_DIM = 64
RW_WIDTH = 512
RW_HEADS = RW_WIDTH // RW_HEAD_DIM
RW_DECAY_LORA = 64
RW_AAA_LORA = 64
RW_GATE_LORA = 128
RW_LN_EPS = 64e-5
RW_PROJ = 3 * RW_WIDTH + RW_DECAY_LORA + RW_AAA_LORA + RW_GATE_LORA
DA_HEAD_DIM = 64
DA_V_DIM = 2 * DA_HEAD_DIM
DA_HEADS = 4
DA_QK = DA_HEADS * 2 * DA_HEAD_DIM
DA_WIDTH = DA_HEADS * DA_V_DIM
ALIBI_MAX_BIAS = 8.0
D_FF = 2816
NORM_EPS = 1e-5
SUBLN_EPS = 1e-5
NEG_BIG = -1e30

VMEM_LIMIT_BYTES = 56 * 1024 * 1024

NT_DIMS = (((1,), (1,)), ((), ()))
TN_DIMS = (((0,), (0,)), ((), ()))


def _params(*semantics):
    return pltpu.CompilerParams(dimension_semantics=semantics,
                                vmem_limit_bytes=VMEM_LIMIT_BYTES)


def _rms(x, g, eps):
    return x * lax.rsqrt(jnp.mean(x * x, axis=-1, keepdims=True) + eps) * g


def _dot_hi(a, b, dims=None):
    if dims is None:
        return jnp.dot(a, b, precision=HIGHEST, preferred_element_type=F32)
    return lax.dot_general(a, b, dims, precision=HIGHEST, preferred_element_type=F32)


def _inproj_body(x_ref, g_ref, w_ref, prw_ref, q_ref, k_ref, v_ref):
    h = _rms(x_ref[...], g_ref[...], NORM_EPS).astype(BF16)
    c0, c1, c2 = RW_PROJ, RW_PROJ + DA_QK, RW_PROJ + 2 * DA_QK
    prw_ref[...] = jnp.dot(h, w_ref[:, :c0], preferred_element_type=F32)
    q_ref[...] = jnp.dot(h, w_ref[:, c0:c1], preferred_element_type=F32)
    k_ref[...] = jnp.dot(h, w_ref[:, c1:c2], preferred_element_type=F32)
    v_ref[...] = jnp.dot(h, w_ref[:, c2:], preferred_element_type=F32)


def _inproj(x2d, g, w_bf16):
    n = x2d.shape[0]
    tm = min(512, n)
    in_proj = w_bf16.shape[1]
    row = lambda width: pl.BlockSpec((tm, width), lambda i: (i, 0))
    return pl.pallas_call(
        _inproj_body,
        grid=(n // tm,),
        in_specs=[row(D_MODEL),
                  pl.BlockSpec((1, D_MODEL), lambda i: (0, 0)),
                  pl.BlockSpec((D_MODEL, in_proj), lambda i: (0, 0))],
        out_specs=[row(RW_PROJ), row(DA_QK), row(DA_QK), row(DA_WIDTH)],
        out_shape=[jax.ShapeDtypeStruct((n, RW_PROJ), F32),
                   jax.ShapeDtypeStruct((n, DA_QK), F32),
                   jax.ShapeDtypeStruct((n, DA_QK), F32),
                   jax.ShapeDtypeStruct((n, DA_WIDTH), F32)],
        compiler_params=_params("parallel"),
        name="inproj",
    )(x2d, g.reshape(1, D_MODEL), w_bf16)


def _softplus(z):
    return jnp.maximum(z, 0.0) + jnp.log(1.0 + jnp.exp(-jnp.abs(z)))


def _prep_body(p_ref, prev_ref, mu_ref, w0_ref, w2_ref, a0_ref, a2_ref, g2_ref,
               kk_ref, ka_ref, rk_ref, cum_ref, tot_ref, hsum_ref,
               at_ref, bt_ref, kt_ref, rt_ref, bh_ref, kh_ref, v_ref, bonus_ref, g_ref, ptot_ref,
               carry_ref):
    i = pl.program_id(1)
    p = p_ref[...]
    tm = p.shape[0]

    @pl.when(i == 0)
    def _():
        carry_ref[...] = prev_ref[0]

    prev_row = carry_ref[...]
    rolled = pltpu.roll(p, 1, axis=0)
    row = lax.broadcasted_iota(jnp.int32, (tm, 1), 0)
    prev = jnp.where(row == 0, prev_row, rolled)
    carry_ref[...] = p[tm - 1:tm, :]
    xs = p + (prev - p) * mu_ref[...]

    c1, c2, c3 = RW_WIDTH, 2 * RW_WIDTH, 3 * RW_WIDTH
    c4 = c3 + RW_DECAY_LORA
    c5 = c4 + RW_AAA_LORA
    r, k, v = xs[:, :c1], xs[:, c1:c2], xs[:, c2:c3]
    wd, ad, gd = xs[:, c3:c4], xs[:, c4:c5], xs[:, c5:]

    w = -_softplus(-(w0_ref[...] + jnp.dot(jnp.tanh(wd), w2_ref[...], preferred_element_type=F32))) - 0.5
    lw = -jnp.exp(w)
    a = jax.nn.sigmoid(a0_ref[...] + jnp.dot(ad, a2_ref[...], preferred_element_type=F32))
    g_ref[...] = jnp.dot(jax.nn.sigmoid(gd), g2_ref[...], preferred_element_type=F32)

    hsum = hsum_ref[...]
    kk = k * kk_ref[...]
    kk = kk / jnp.maximum(jnp.sqrt(_dot_hi(kk * kk, hsum)), 1e-12)
    k_h = k * (1.0 + (a - 1.0) * ka_ref[...])
    bonus_ref[...] = _dot_hi(r * k_h * rk_ref[...], hsum) * v
    v_ref[...] = v

    cum = _dot_hi(cum_ref[...], lw)
    tot = _dot_hi(tot_ref[...], lw)
    e_neg = jnp.exp(-cum)
    e_rest = jnp.exp(tot - cum)
    b = kk * a
    at_ref[...] = -kk * jnp.exp(cum - lw)
    bt_ref[...] = b * e_neg
    kt_ref[...] = k_h * e_neg
    rt_ref[...] = r * jnp.exp(cum)
    bh_ref[...] = b * e_rest
    kh_ref[...] = k_h * e_rest
    ptot_ref[...] = jnp.exp(tot)


def _chunk_matrices(tm, chunk):
    idx = np.arange(tm)
    same = (idx[:, None] // chunk) == (idx[None, :] // chunk)
    cum = (same & (idx[None, :] <= idx[:, None])).astype(np.float32)
    return jnp.asarray(cum), jnp.asarray(same.astype(np.float32))


def _head_sum_matrix(scale=1.0):
    idx = np.arange(RW_WIDTH) // RW_HEAD_DIM
    return jnp.asarray((idx[:, None] == idx[None, :]).astype(np.float32) * scale)


def _rwkv_prep(p_rw, prev_rows, batch, seq, chunk, lp):
    n = batch * seq
    tm = min(256, seq)
    nt = seq // tm
    cum_m, tot_m = _chunk_matrices(tm, chunk)
    vec = lambda width: pl.BlockSpec((1, width), lambda b, i: (0, 0))
    full = lambda r, c: pl.BlockSpec((r, c), lambda b, i: (0, 0))
    tok = pl.BlockSpec((tm, RW_WIDTH), lambda b, i: (b * nt + i, 0))
    out = jax.ShapeDtypeStruct((n, RW_WIDTH), F32)
    return pl.pallas_call(
        _prep_body,
        grid=(batch, nt),
        in_specs=[pl.BlockSpec((tm, RW_PROJ), lambda b, i: (b * nt + i, 0)),
                  pl.BlockSpec((1, 1, RW_PROJ), lambda b, i: (b, 0, 0)),
                  vec(RW_PROJ), vec(RW_WIDTH), full(RW_DECAY_LORA, RW_WIDTH),
                  vec(RW_WIDTH), full(RW_AAA_LORA, RW_WIDTH), full(RW_GATE_LORA, RW_WIDTH),
                  vec(RW_WIDTH), vec(RW_WIDTH), vec(RW_WIDTH),
                  full(tm, tm), full(tm, tm), full(RW_WIDTH, RW_WIDTH)],
        out_specs=[tok] * 10,
        out_shape=[out] * 10,
        scratch_shapes=[pltpu.VMEM((1, RW_PROJ), F32)],
        compiler_params=_params("parallel", "arbitrary"),
        name="rwkv_prep",
    )(p_rw, prev_rows.reshape(batch, 1, RW_PROJ),
      lp["mu"].reshape(1, RW_PROJ), lp["w0"].reshape(1, RW_WIDTH), lp["w2"],
      lp["a0"].reshape(1, RW_WIDTH), lp["a2"], lp["g2"],
      lp["k_k"].reshape(1, RW_WIDTH), lp["k_a"].reshape(1, RW_WIDTH), lp["r_k"].reshape(1, RW_WIDTH),
      cum_m, tot_m, _head_sum_matrix())


def _intra_body(at_ref, bt_ref, kt_ref, rt_ref, bh_ref, kh_ref, v_ref, ptot_ref,
                qt_ref, y0_ref, mt_ref, nt_ref):
    c = at_ref.shape[0]
    row = lax.broadcasted_iota(jnp.int32, (c, c), 0)
    col = lax.broadcasted_iota(jnp.int32, (c, c), 1)
    strict = row > col
    incl = row >= col
    eye = (row == col).astype(F32)
    krow = lax.broadcasted_iota(jnp.int32, (RW_HEAD_DIM, RW_HEAD_DIM), 0)
    kcol = lax.broadcasted_iota(jnp.int32, (RW_HEAD_DIM, RW_HEAD_DIM), 1)
    steps = int(math.log2(c))
    for h in range(RW_HEADS):
        sl = slice(h * RW_HEAD_DIM, (h + 1) * RW_HEAD_DIM)
        a, b, k, r = at_ref[:, sl], bt_ref[:, sl], kt_ref[:, sl], rt_ref[:, sl]
        bh, kh, v = bh_ref[:, sl], kh_ref[:, sl], v_ref[:, sl]
        a_ab = jnp.where(strict, _dot_hi(a, b, NT_DIMS), 0.0)
        a_ak = jnp.where(strict, _dot_hi(a, k, NT_DIMS), 0.0)
        a_rb = jnp.where(incl, _dot_hi(r, b, NT_DIMS), 0.0)
        a_rk = jnp.where(incl, _dot_hi(r, k, NT_DIMS), 0.0)
        inv = eye + a_ab
        power = a_ab
        for _ in range(steps - 1):
            power = _dot_hi(power, power)
            inv = inv + _dot_hi(inv, power)
        w1 = _dot_hi(inv, a)
        w2 = _dot_hi(inv, _dot_hi(a_ak, v))
        qt_ref[:, sl] = r + _dot_hi(a_rb, w1)
        y0_ref[:, sl] = _dot_hi(a_rk, v) + _dot_hi(a_rb, w2)
        ptot = jnp.broadcast_to(ptot_ref[0:1, sl], (RW_HEAD_DIM, RW_HEAD_DIM))
        mt_ref[:, sl] = jnp.where(krow == kcol, ptot, 0.0) + _dot_hi(w1, bh, TN_DIMS)
        nt_ref[:, sl] = _dot_hi(w2, bh, TN_DIMS) + _dot_hi(v, kh, TN_DIMS)


def _rwkv_intra(prep, n, chunk):
    at, bt, kt, rt, bh, kh, v, _, _, ptot = prep
    nchunks = n // chunk
    tok = pl.BlockSpec((chunk, RW_WIDTH), lambda i: (i, 0))
    mat = pl.BlockSpec((RW_HEAD_DIM, RW_WIDTH), lambda i: (i, 0))
    return pl.pallas_call(
        _intra_body,
        grid=(nchunks,),
        in_specs=[tok] * 8,
        out_specs=[tok, tok, mat, mat],
        out_shape=[jax.ShapeDtypeStruct((n, RW_WIDTH), F32),
                   jax.ShapeDtypeStruct((n, RW_WIDTH), F32),
                   jax.ShapeDtypeStruct((nchunks * RW_HEAD_DIM, RW_WIDTH), F32),
                   jax.ShapeDtypeStruct((nchunks * RW_HEAD_DIM, RW_WIDTH), F32)],
        compiler_params=_params("parallel"),
        name="rwkv_intra",
    )(at, bt, kt, rt, bh, kh, v, ptot)


def _inter_body(qt_ref, y0_ref, mt_ref, nt_ref, bonus_ref, g_ref, s0_ref, lnw_ref, lnb_ref, hmean_ref,
                o_ref, sout_ref, s_ref):
    ci = pl.program_id(1)

    @pl.when(ci == 0)
    def _():
        s_ref[...] = s0_ref[0]

    ys = []
    for h in range(RW_HEADS):
        sl = slice(h * RW_HEAD_DIM, (h + 1) * RW_HEAD_DIM)
        s = s_ref[h]
        ys.append(y0_ref[:, sl] + _dot_hi(qt_ref[:, sl], s, NT_DIMS))
        s_ref[h] = _dot_hi(s, mt_ref[:, sl]) + nt_ref[:, sl]
    y = jnp.concatenate(ys, axis=-1)
    hmean = hmean_ref[...]
    mean = _dot_hi(y, hmean)
    d = y - mean
    var = _dot_hi(d * d, hmean)
    y = d * lax.rsqrt(var + RW_LN_EPS) * lnw_ref[...] + lnb_ref[...]
    o_ref[...] = (y + bonus_ref[...]) * g_ref[...]

    @pl.when(ci == pl.num_programs(1) - 1)
    def _():
        sout_ref[0] = s_ref[...]


def _rwkv_inter(qt, y0, mt, nt, bonus, g, s0, lnx_w, lnx_b, batch, seq, chunk):
    n = batch * seq
    nc = seq // chunk
    tok = pl.BlockSpec((chunk, RW_WIDTH), lambda b, c: (b * nc + c, 0))
    mat = pl.BlockSpec((RW_HEAD_DIM, RW_WIDTH), lambda b, c: (b * nc + c, 0))
    state = pl.BlockSpec((1, RW_HEADS, RW_HEAD_DIM, RW_HEAD_DIM), lambda b, c: (b, 0, 0, 0))
    vec = pl.BlockSpec((1, RW_WIDTH), lambda b, c: (0, 0))
    return pl.pallas_call(
        _inter_body,
        grid=(batch, nc),
        in_specs=[tok, tok, mat, mat, tok, tok, state, vec, vec,
                  pl.BlockSpec((RW_WIDTH, RW_WIDTH), lambda b, c: (0, 0))],
        out_specs=[tok, state],
        out_shape=[jax.ShapeDtypeStruct((n, RW_WIDTH), F32),
                   jax.ShapeDtypeStruct((batch, RW_HEADS, RW_HEAD_DIM, RW_HEAD_DIM), F32)],
        scratch_shapes=[pltpu.VMEM((RW_HEADS, RW_HEAD_DIM, RW_HEAD_DIM), F32)],
        compiler_params=_params("parallel", "arbitrary"),
        name="rwkv_inter",
    )(qt, y0, mt, nt, bonus, g, s0, lnx_w.reshape(1, RW_WIDTH), lnx_b.reshape(1, RW_WIDTH),
      _head_sum_matrix(1.0 / RW_HEAD_DIM))


def _rwkv_mixer(p_rw, prev_rows, s0, batch, seq, lp):
    chunk = min(64, seq)
    prep = _rwkv_prep(p_rw, prev_rows, batch, seq, chunk, lp)
    qt, y0, mt, nt = _rwkv_intra(prep, batch * seq, chunk)
    return _rwkv_inter(qt, y0, mt, nt, prep[7], prep[8], s0, lp["lnx_w"], lp["lnx_b"], batch, seq, chunk)


def _lam_of(lam_ref, lam_init):
    lq1, lk1, lq2, lk2 = lam_ref[0:1, :], lam_ref[1:2, :], lam_ref[2:3, :], lam_ref[3:4, :]
    return (jnp.exp(jnp.sum(lq1 * lk1, axis=-1, keepdims=True))
            - jnp.exp(jnp.sum(lq2 * lk2, axis=-1, keepdims=True)) + lam_init)


def _diff_finish(acc1, l1, acc2, l2, lam, subln_w, lam_init):
    o = acc1 / l1 - lam * (acc2 / l2)
    o = o * lax.rsqrt(jnp.mean(o * o, axis=-1, keepdims=True) + SUBLN_EPS) * subln_w
    return o * (1.0 - lam_init)


def _attn_prompt_body(qi_ref, kj_ref, slopes_ref, q_ref, k_ref, v_ref, lam_ref, subln_ref, o_ref,
                      m_ref, l_ref, acc_ref, *, lam_init, tq, tk):
    h = pl.program_id(1)
    t = pl.program_id(2)
    i = qi_ref[t]
    j = kj_ref[t]

    @pl.when(j == 0)
    def _():
        m_ref[...] = jnp.full(m_ref.shape, NEG_BIG, F32)
        l_ref[...] = jnp.zeros(l_ref.shape, F32)
        acc_ref[...] = jnp.zeros(acc_ref.shape, F32)

    scale = DA_HEAD_DIM ** -0.5
    q = q_ref[...] * scale
    k = k_ref[...]
    vb = v_ref[...].astype(BF16)
    rows = lax.broadcasted_iota(jnp.int32, (tq, tk), 0)
    cols = lax.broadcasted_iota(jnp.int32, (tq, tk), 1)
    dist = (rows - cols) + (i * tq - j * tk)
    bias = dist.astype(F32) * (-slopes_ref[h])
    for c in range(2):
        sl = slice(c * DA_HEAD_DIM, (c + 1) * DA_HEAD_DIM)
        s = lax.dot_general(q[:, sl].astype(BF16), k[:, sl].astype(BF16), NT_DIMS,
                            preferred_element_type=F32) + bias
        s = jnp.where(dist >= 0, s, NEG_BIG)
        m_prev = m_ref[c]
        m_new = jnp.maximum(m_prev, jnp.max(s, axis=-1, keepdims=True))
        alpha = jnp.exp(m_prev - m_new)
        p = jnp.exp(s - m_new)
        l_ref[c] = alpha * l_ref[c] + jnp.sum(p, axis=-1, keepdims=True)
        acc_ref[c] = alpha * acc_ref[c] + jnp.dot(p.astype(BF16), vb, preferred_element_type=F32)
        m_ref[c] = m_new

    @pl.when(j == i)
    def _():
        lam = _lam_of(lam_ref, lam_init)
        o_ref[...] = _diff_finish(acc_ref[0], l_ref[0], acc_ref[1], l_ref[1], lam, subln_ref[...], lam_init)


def _alibi_slopes():
    return jnp.exp2(-ALIBI_MAX_BIAS / DA_HEADS * jnp.arange(1, DA_HEADS + 1, dtype=F32))


def _attn_prompt(q, k, v, lam_params, subln_w, lam_init, batch, seq):
    n = batch * seq
    tq = tk = min(512, seq)
    nq = seq // tq
    pairs = [(i, j) for i in range(nq) for j in range(i + 1)]
    qi = jnp.asarray([pr[0] for pr in pairs], jnp.int32)
    kj = jnp.asarray([pr[1] for pr in pairs], jnp.int32)
    body = functools.partial(_attn_prompt_body, lam_init=lam_init, tq=tq, tk=tk)
    grid_spec = pltpu.PrefetchScalarGridSpec(
        num_scalar_prefetch=3,
        grid=(batch, DA_HEADS, len(pairs)),
        in_specs=[pl.BlockSpec((tq, DA_V_DIM), lambda b, h, t, qi, kj, sl: (b * nq + qi[t], h)),
                  pl.BlockSpec((tk, DA_V_DIM), lambda b, h, t, qi, kj, sl: (b * nq + kj[t], h)),
                  pl.BlockSpec((tk, DA_V_DIM), lambda b, h, t, qi, kj, sl: (b * nq + kj[t], h)),
                  pl.BlockSpec((4, DA_HEAD_DIM), lambda b, h, t, qi, kj, sl: (0, 0)),
                  pl.BlockSpec((1, DA_V_DIM), lambda b, h, t, qi, kj, sl: (0, 0))],
        out_specs=pl.BlockSpec((tq, DA_V_DIM), lambda b, h, t, qi, kj, sl: (b * nq + qi[t], h)),
        scratch_shapes=[pltpu.VMEM((2, tq, 1), F32), pltpu.VMEM((2, tq, 1), F32),
                        pltpu.VMEM((2, tq, DA_V_DIM), F32)],
    )
    return pl.pallas_call(
        body,
        grid_spec=grid_spec,
        out_shape=jax.ShapeDtypeStruct((n, DA_WIDTH), F32),
        compiler_params=_params("parallel", "parallel", "arbitrary"),
        name="attn_prompt",
    )(qi, kj, _alibi_slopes(), q, k, v, lam_params, subln_w.reshape(1, DA_V_DIM))


def _attn_sample_body(pt_ref, slopes_ref, q_ref, kn_ref, vn_ref, lam_ref, subln_ref, *rest,
                      lam_init, layer, pages_per_step, page_size, n_past, n_new):
    kp_refs = rest[:pages_per_step]
    vp_refs = rest[pages_per_step:2 * pages_per_step]
    o_ref = rest[2 * pages_per_step]
    qbd_ref, m_ref, l_ref, acc_ref = rest[2 * pages_per_step + 1:]
    del pt_ref, layer
    j = pl.program_id(1)
    rows2 = 2 * n_new
    scale = DA_HEAD_DIM ** -0.5

    @pl.when(j == 0)
    def _():
        q = q_ref[...] * scale
        lane = lax.broadcasted_iota(jnp.int32, (n_new, DA_V_DIM), 1)
        for h in range(DA_HEADS):
            qh = q[:, h * DA_V_DIM:(h + 1) * DA_V_DIM]
            top = jnp.where(lane < DA_HEAD_DIM, qh, 0.0)
            bot = jnp.where(lane >= DA_HEAD_DIM, qh, 0.0)
            qbd_ref[h] = jnp.concatenate([top, bot], axis=0)
        m_ref[...] = jnp.full(m_ref.shape, NEG_BIG, F32)
        l_ref[...] = jnp.zeros(l_ref.shape, F32)
        acc_ref[...] = jnp.zeros(acc_ref.shape, F32)

    qpos = n_past + (lax.broadcasted_iota(jnp.int32, (rows2, page_size), 0) % n_new)
    kpos0 = lax.broadcasted_iota(jnp.int32, (rows2, page_size), 1) + j * (pages_per_step * page_size)
    for h in range(DA_HEADS):
        hs = slice(h * DA_V_DIM, (h + 1) * DA_V_DIM)
        qbd = qbd_ref[h].astype(BF16)
        slope = slopes_ref[h]
        ss = []
        for u in range(pages_per_step):
            kp = kp_refs[u][0, 0, :, hs].astype(BF16)
            s = lax.dot_general(qbd, kp, NT_DIMS, preferred_element_type=F32)
            dist = (qpos - (kpos0 + u * page_size)).astype(F32)
            ss.append(s - slope * dist)
        smax = functools.reduce(jnp.maximum, ss)
        m_prev = m_ref[h]
        m_new = jnp.maximum(m_prev, jnp.max(smax, axis=-1, keepdims=True))
        alpha = jnp.exp(m_prev - m_new)
        psum = None
        pv = None
        for u in range(pages_per_step):
            p = jnp.exp(ss[u] - m_new)
            psum = p if psum is None else psum + p
            d = jnp.dot(p.astype(BF16), vp_refs[u][0, 0, :, hs].astype(BF16), preferred_element_type=F32)
            pv = d if pv is None else pv + d
        l_ref[h] = alpha * l_ref[h] + jnp.sum(psum, axis=-1, keepdims=True)
        acc_ref[h] = alpha * acc_ref[h] + pv
        m_ref[h] = m_new

    @pl.when(j == pl.num_programs(1) - 1)
    def _():
        lam = _lam_of(lam_ref, lam_init)
        qn = lax.broadcasted_iota(jnp.int32, (rows2, n_new), 0) % n_new
        kn = lax.broadcasted_iota(jnp.int32, (rows2, n_new), 1)
        dist_new = qn - kn
        for h in range(DA_HEADS):
            hs = slice(h * DA_V_DIM, (h + 1) * DA_V_DIM)
            s = lax.dot_general(qbd_ref[h], kn_ref[:, hs], NT_DIMS, preferred_element_type=F32)
            s = s - slopes_ref[h] * dist_new.astype(F32)
            s = jnp.where(dist_new >= 0, s, NEG_BIG)
            m_prev = m_ref[h]
            m_new = jnp.maximum(m_prev, jnp.max(s, axis=-1, keepdims=True))
            alpha = jnp.exp(m_prev - m_new)
            p = jnp.exp(s - m_new)
            l = alpha * l_ref[h] + jnp.sum(p, axis=-1, keepdims=True)
            acc = alpha * acc_ref[h] + jnp.dot(p, vn_ref[:, hs], preferred_element_type=F32)
            o_ref[:, hs] = _diff_finish(acc[:n_new], l[:n_new], acc[n_new:], l[n_new:], lam,
                                        subln_ref[...], lam_init)


def _attn_sample(q, k_new, v_new, cache_k, cache_v, page_table, lam_params, subln_w, lam_init, layer,
                 dec_batch, dec_seq):
    page_size = cache_k.shape[2]
    n_pages = page_table.shape[1]
    pps = 8 if n_pages % 8 == 0 else 1
    n_past = n_pages * page_size
    body = functools.partial(_attn_sample_body, lam_init=lam_init, layer=layer, pages_per_step=pps,
                             page_size=page_size, n_past=n_past, n_new=dec_seq)
    tok = pl.BlockSpec((dec_seq, DA_QK), lambda b, j, pt, sl: (b, 0))

    def page_spec(u):
        return pl.BlockSpec((1, 1, page_size, DA_QK),
                            lambda b, j, pt, sl: (layer, pt[b, j * pps + u], 0, 0))

    grid_spec = pltpu.PrefetchScalarGridSpec(
        num_scalar_prefetch=2,
        grid=(dec_batch, n_pages // pps),
        in_specs=[tok, tok, tok,
                  pl.BlockSpec((4, DA_HEAD_DIM), lambda b, j, pt, sl: (0, 0)),
                  pl.BlockSpec((1, DA_V_DIM), lambda b, j, pt, sl: (0, 0))]
                 + [page_spec(u) for u in range(pps)] + [page_spec(u) for u in range(pps)],
        out_specs=tok,
        scratch_shapes=[pltpu.VMEM((DA_HEADS, 2 * dec_seq, DA_V_DIM), F32),
                        pltpu.VMEM((DA_HEADS, 2 * dec_seq, 1), F32),
                        pltpu.VMEM((DA_HEADS, 2 * dec_seq, 1), F32),
                        pltpu.VMEM((DA_HEADS, 2 * dec_seq, DA_V_DIM), F32)],
    )
    return pl.pallas_call(
        body,
        grid_spec=grid_spec,
        out_shape=jax.ShapeDtypeStruct((dec_batch * dec_seq, DA_WIDTH), F32),
        compiler_params=_params("parallel", "arbitrary"),
        name="attn_sample",
    )(page_table, _alibi_slopes(), q, k_new, v_new, lam_params, subln_w.reshape(1, DA_V_DIM),
      *([cache_k] * pps), *([cache_v] * pps))


FFN_CHUNK = 256


def _ffn_body(x_ref, orw_ref, oda_ref, wo_ref, g2_ref, w1_ref, w3_ref, w2_ref, fg_ref, out_ref, *, final):
    x1 = (x_ref[...]
          + jnp.dot(orw_ref[...].astype(BF16), wo_ref[:RW_WIDTH, :], preferred_element_type=F32)
          + jnp.dot(oda_ref[...].astype(BF16), wo_ref[RW_WIDTH:, :], preferred_element_type=F32))
    hf = _rms(x1, g2_ref[...], NORM_EPS).astype(BF16)
    acc = jnp.zeros_like(x1)
    for c0 in range(0, D_FF, FFN_CHUNK):
        a = jnp.dot(hf, w1_ref[:, c0:c0 + FFN_CHUNK], preferred_element_type=F32)
        b = jnp.dot(hf, w3_ref[:, c0:c0 + FFN_CHUNK], preferred_element_type=F32)
        z = (a * jax.nn.sigmoid(a) * b).astype(BF16)
        acc = acc + jnp.dot(z, w2_ref[c0:c0 + FFN_CHUNK, :], preferred_element_type=F32)
    x2 = x1 + acc
    out_ref[...] = _rms(x2, fg_ref[...], NORM_EPS) if final else x2


def _outproj_ffn(x2d, o_rw, o_da, wo, g2, w1, w3, w2, final_g, final):
    n = x2d.shape[0]
    tm = min(512, n)
    row = lambda width: pl.BlockSpec((tm, width), lambda i: (i, 0))
    full = lambda r, c: pl.BlockSpec((r, c), lambda i: (0, 0))
    return pl.pallas_call(
        functools.partial(_ffn_body, final=final),
        grid=(n // tm,),
        in_specs=[row(D_MODEL), row(RW_WIDTH), row(DA_WIDTH), full(D_MODEL, D_MODEL), full(1, D_MODEL),
                  full(D_MODEL, D_FF), full(D_MODEL, D_FF), full(D_FF, D_MODEL), full(1, D_MODEL)],
        out_specs=row(D_MODEL),
        out_shape=jax.ShapeDtypeStruct((n, D_MODEL), F32),
        compiler_params=_params("parallel"),
        name="outproj_ffn",
    )(x2d, o_rw, o_da, wo, g2.reshape(1, D_MODEL), w1, w3, w2, final_g.reshape(1, D_MODEL))


def _lambda_init(layer):
    return 0.8 - 0.6 * math.exp(-0.3 * layer)


def kernel(x_prompt, x_sample, cache_k, cache_v, state_wkv, state_shift, page_table, norm1_g, w_in, rw_mu, rw_w0, rw_w2, rw_a0, rw_a2, rw_g2, rw_k_k, rw_k_a, rw_r_k, rw_lnx_w, rw_lnx_b, da_lam_q1, da_lam_k1, da_lam_q2, da_lam_k2, da_subln_w, w_out, norm2_g, ffn_w1, ffn_w3, ffn_w2, final_g):
    depth = w_in.shape[0]
    batch, seq, _ = x_prompt.shape
    dec_batch, dec_seq, _ = x_sample.shape
    n_pool, page_size = cache_k.shape[1], cache_k.shape[2]
    ck = cache_k.reshape(depth, n_pool, page_size, DA_QK)
    cv = cache_v.reshape(depth, n_pool, page_size, DA_WIDTH)

    xp = x_prompt.reshape(batch * seq, D_MODEL)
    xs = x_sample.reshape(dec_batch * dec_seq, D_MODEL)
    outs = {name: [] for name in ("kp", "vp", "sp", "shp", "ks", "vs", "ss", "shs")}
    for l in range(depth):
        lam_init = _lambda_init(l)
        lp = dict(mu=rw_mu[l], w0=rw_w0[l], w2=rw_w2[l], a0=rw_a0[l], a2=rw_a2[l], g2=rw_g2[l],
                  k_k=rw_k_k[l], k_a=rw_k_a[l], r_k=rw_r_k[l].reshape(RW_WIDTH),
                  lnx_w=rw_lnx_w[l], lnx_b=rw_lnx_b[l])
        lam_params = jnp.stack([da_lam_q1[l], da_lam_k1[l], da_lam_q2[l], da_lam_k2[l]])
        w_in_b = w_in[l].astype(BF16)
        wo_b, w1_b, w3_b, w2_b = (w_out[l].astype(BF16), ffn_w1[l].astype(BF16),
                                  ffn_w3[l].astype(BF16), ffn_w2[l].astype(BF16))
        final = l == depth - 1

        p_rw, q, k, v = _inproj(xp, norm1_g[l], w_in_b)
        o_rw, s_new = _rwkv_mixer(p_rw, jnp.zeros((batch, RW_PROJ), F32),
                                  jnp.zeros((batch, RW_HEADS, RW_HEAD_DIM, RW_HEAD_DIM), F32),
                                  batch, seq, lp)
        o_da = _attn_prompt(q, k, v, lam_params, da_subln_w[l], lam_init, batch, seq)
        xp = _outproj_ffn(xp, o_rw, o_da, wo_b, norm2_g[l], w1_b, w3_b, w2_b, final_g, final)
        outs["kp"].append(k.reshape(batch, seq, DA_HEADS, 2 * DA_HEAD_DIM))
        outs["vp"].append(v.reshape(batch, seq, DA_HEADS, DA_V_DIM))
        outs["sp"].append(s_new)
        outs["shp"].append(p_rw.reshape(batch, seq, RW_PROJ)[:, -1])

        p_rw, q, k, v = _inproj(xs, norm1_g[l], w_in_b)
        o_rw, s_new = _rwkv_mixer(p_rw, state_shift[l], state_wkv[l], dec_batch, dec_seq, lp)
        o_da = _attn_sample(q, k, v, ck, cv, page_table, lam_params, da_subln_w[l], lam_init, l,
                            dec_batch, dec_seq)
        xs = _outproj_ffn(xs, o_rw, o_da, wo_b, norm2_g[l], w1_b, w3_b, w2_b, final_g, final)
        outs["ks"].append(k.reshape(dec_batch, dec_seq, DA_HEADS, 2 * DA_HEAD_DIM))
        outs["vs"].append(v.reshape(dec_batch, dec_seq, DA_HEADS, DA_V_DIM))
        outs["ss"].append(s_new)
        outs["shs"].append(p_rw.reshape(dec_batch, dec_seq, RW_PROJ)[:, -1])

    return (xp.reshape(batch, seq, D_MODEL), xs.reshape(dec_batch, dec_seq, D_MODEL),
            jnp.stack(outs["kp"]), jnp.stack(outs["vp"]), jnp.stack(outs["sp"]), jnp.stack(outs["shp"]),
            jnp.stack(outs["ks"]), jnp.stack(outs["vs"]), jnp.stack(outs["ss"]), jnp.stack(outs["shs"]))
```

```python
import functools
import math

import jax
import jax.numpy as jnp
import numpy as np
from jax import lax
from jax.experimental import pallas as pl
from jax.experimental.pallas import tpu as pltpu

F32 = jnp.float32
BF16 = jnp.bfloat16

D_MODEL = 1024
RW_HEAD_DIM = 64
RW_WIDTH = 512
RW_HEADS = RW_WIDTH // RW_HEAD_DIM
RW_DECAY_LORA = 64
RW_AAA_LORA = 64
RW_GATE_LORA = 128
RW_LN_EPS = 64e-5
RW_PROJ = 3 * RW_WIDTH + RW_DECAY_LORA + RW_AAA_LORA + RW_GATE_LORA
DA_HEAD_DIM = 64
DA_V_DIM = 2 * DA_HEAD_DIM
DA_HEADS = 4
DA_QK = DA_HEADS * 2 * DA_HEAD_DIM
DA_WIDTH = DA_HEADS * DA_V_DIM
ALIBI_MAX_BIAS = 8.0
D_FF = 2816
NORM_EPS = 1e-5
SUBLN_EPS = 1e-5
NEG_BIG = -1e30
LOG2E = 1.4426950408889634

LANES = 128
RW_PAIRS = RW_WIDTH // LANES
RW_CHUNK = 64
RW_GROUP = 2

VMEM_LIMIT_BYTES = 56 * 1024 * 1024

NT_DIMS = (((1,), (1,)), ((), ()))
TN_DIMS = (((0,), (0,)), ((), ()))


def _params(*semantics):
    return pltpu.CompilerParams(dimension_semantics=semantics,
                                vmem_limit_bytes=VMEM_LIMIT_BYTES)


def _rms(x, g, eps):
    return x * lax.rsqrt(jnp.mean(x * x, axis=-1, keepdims=True) + eps) * g


def _mm(a, b, dims=None):
    a, b = a.astype(BF16), b.astype(BF16)
    if dims is None:
        return jnp.dot(a, b, preferred_element_type=F32)
    return lax.dot_general(a, b, dims, preferred_element_type=F32)


def _split_terms(x, terms):
    out = []
    for _ in range(terms):
        hi = x.astype(BF16)
        out.append(hi)
        x = x - hi.astype(F32)
    return out


def _inproj_body(x_ref, g_ref, w_ref, prw_ref, q_ref, k_ref, v_ref):
    h = _rms(x_ref[...], g_ref[...], NORM_EPS).astype(BF16)
    c0, c1, c2 = RW_PROJ, RW_PROJ + DA_QK, RW_PROJ + 2 * DA_QK
    prw_ref[...] = jnp.dot(h, w_ref[:, :c0], preferred_element_type=F32)
    q_ref[...] = jnp.dot(h, w_ref[:, c0:c1], preferred_element_type=F32)
    k_ref[...] = jnp.dot(h, w_ref[:, c1:c2], preferred_element_type=F32)
    v_ref[...] = jnp.dot(h, w_ref[:, c2:], preferred_element_type=F32)


def _inproj_prompt_body(x_ref, g_ref, w_ref, wvt_ref, prw_ref, k_ref, v_ref, qb_ref, kb_ref, vtb_ref):
    h = _rms(x_ref[...], g_ref[...], NORM_EPS).astype(BF16)
    c0, c1, c2 = RW_PROJ, RW_PROJ + DA_QK, RW_PROJ + 2 * DA_QK
    prw_ref[...] = jnp.dot(h, w_ref[:, :c0], preferred_element_type=F32)
    q = jnp.dot(h, w_ref[:, c0:c1], preferred_element_type=F32)
    qb_ref[...] = (q * (DA_HEAD_DIM ** -0.5 * LOG2E)).astype(BF16)
    k = jnp.dot(h, w_ref[:, c1:c2], preferred_element_type=F32)
    k_ref[...] = k
    kb_ref[...] = k.astype(BF16)
    v_ref[...] = jnp.dot(h, w_ref[:, c2:], preferred_element_type=F32)
    vtb_ref[...] = lax.dot_general(wvt_ref[...], h, NT_DIMS, preferred_element_type=F32).astype(BF16)


def _inproj_prompt(x2d, g, w_bf16):
    n = x2d.shape[0]
    tm = min(512, n)
    in_proj = w_bf16.shape[1]
    wvt = w_bf16[:, RW_PROJ + 2 * DA_QK:].T
    row = lambda width: pl.BlockSpec((tm, width), lambda i: (i, 0))
    return pl.pallas_call(
        _inproj_prompt_body,
        grid=(n // tm,),
        in_specs=[row(D_MODEL),
                  pl.BlockSpec((1, D_MODEL), lambda i: (0, 0)),
                  pl.BlockSpec((D_MODEL, in_proj), lambda i: (0, 0)),
                  pl.BlockSpec((DA_WIDTH, D_MODEL), lambda i: (0, 0))],
        out_specs=[row(RW_PROJ), row(DA_QK), row(DA_WIDTH), row(DA_QK), row(DA_QK),
                   pl.BlockSpec((DA_WIDTH, tm), lambda i: (0, i))],
        out_shape=[jax.ShapeDtypeStruct((n, RW_PROJ), F32),
                   jax.ShapeDtypeStruct((n, DA_QK), F32),
                   jax.ShapeDtypeStruct((n, DA_WIDTH), F32),
                   jax.ShapeDtypeStruct((n, DA_QK), BF16),
                   jax.ShapeDtypeStruct((n, DA_QK), BF16),
                   jax.ShapeDtypeStruct((DA_WIDTH, n), BF16)],
        compiler_params=_params("parallel"),
        name="inproj_prompt",
    )(x2d, g.reshape(1, D_MODEL), w_bf16, wvt)


def _inproj(x2d, g, w_bf16):
    n = x2d.shape[0]
    tm = min(512, n)
    in_proj = w_bf16.shape[1]
    row = lambda width: pl.BlockSpec((tm, width), lambda i: (i, 0))
    return pl.pallas_call(
        _inproj_body,
        grid=(n // tm,),
        in_specs=[row(D_MODEL),
                  pl.BlockSpec((1, D_MODEL), lambda i: (0, 0)),
                  pl.BlockSpec((D_MODEL, in_proj), lambda i: (0, 0))],
        out_specs=[row(RW_PROJ), row(DA_QK), row(DA_QK), row(DA_WIDTH)],
        out_shape=[jax.ShapeDtypeStruct((n, RW_PROJ), F32),
                   jax.ShapeDtypeStruct((n, DA_QK), F32),
                   jax.ShapeDtypeStruct((n, DA_QK), F32),
                   jax.ShapeDtypeStruct((n, DA_WIDTH), F32)],
        compiler_params=_params("parallel"),
        name="inproj",
    )(x2d, g.reshape(1, D_MODEL), w_bf16)


def _softplus(z):
    return jnp.maximum(z, 0.0) + jnp.log(1.0 + jnp.exp(-jnp.abs(z)))


def _pair_sums(x, ones_pair):
    cols = []
    for q in range(RW_PAIRS):
        terms = _split_terms(x[:, q * LANES:(q + 1) * LANES], 2)
        cols.append(sum(jnp.dot(t, ones_pair, preferred_element_type=F32) for t in terms))
    return jnp.concatenate(cols, axis=-1)


def _cat0(*xs):
    return jnp.concatenate(xs, axis=0)


def _cat1(*xs):
    return jnp.concatenate(xs, axis=1)


def _rwkv_body(p_ref, prev_ref, s0_ref, mu_ref, w0_ref, w2_ref, a0_ref, a2_ref, g2_ref,
               kk_ref, ka_ref, rk_ref, lnw_ref, lnb_ref, tril_ref, ones_ref,
               o_ref, sout_ref, carry_ref, s_ref, *, tm, valid):
    i = pl.program_id(1)
    c_len = RW_CHUNK
    half = RW_HEAD_DIM
    zeros_hh = jnp.zeros((half, half), F32)

    @pl.when(i == 0)
    def _():
        carry_ref[...] = prev_ref[0]
        for q in range(RW_PAIRS):
            top = _cat1(s0_ref[0, 2 * q], zeros_hh)
            bot = _cat1(zeros_hh, s0_ref[0, 2 * q + 1])
            s_ref[q] = _cat0(top, bot)

    p = p_ref[...]
    prev_row = carry_ref[...]
    carry_ref[...] = p[valid - 1:valid, :]
    if valid < tm:
        p = _cat0(p, jnp.zeros((tm - valid, RW_PROJ), F32))
    rolled = pltpu.roll(p, 1, axis=0)
    row = lax.broadcasted_iota(jnp.int32, (tm, 1), 0)
    prev = jnp.where(row == 0, prev_row, rolled)
    xs = p + (prev - p) * mu_ref[...]

    c1, c2, c3 = RW_WIDTH, 2 * RW_WIDTH, 3 * RW_WIDTH
    c4 = c3 + RW_DECAY_LORA
    c5 = c4 + RW_AAA_LORA
    r, k, v = xs[:, :c1], xs[:, c1:c2], xs[:, c2:c3]
    wd, ad, gd = xs[:, c3:c4], xs[:, c4:c5], xs[:, c5:]

    w = -_softplus(-(w0_ref[...] + jnp.dot(jnp.tanh(wd), w2_ref[...], preferred_element_type=F32))) - 0.5
    lw = -jnp.exp(w)
    a = jax.nn.sigmoid(a0_ref[...] + jnp.dot(ad, a2_ref[...], preferred_element_type=F32))
    g = jnp.dot(jax.nn.sigmoid(gd), g2_ref[...], preferred_element_type=F32)

    ones_pair = ones_ref[...]
    kk = k * kk_ref[...]
    kk = kk / jnp.maximum(jnp.sqrt(_pair_sums(kk * kk, ones_pair)), 1e-12)
    k_h = k * (1.0 + (a - 1.0) * ka_ref[...])
    bonus = _pair_sums(r * k_h * rk_ref[...], ones_pair) * v
    b = kk * a

    if valid < tm:
        live = row < valid
        lw = jnp.where(live, lw, 0.0)
        kk, b, k_h, r, v = (jnp.where(live, t, 0.0) for t in (kk, b, k_h, r, v))

    tril = tril_ref[...]
    cum = sum(jnp.dot(tril, t, preferred_element_type=F32) for t in _split_terms(lw, 3))
    nchunks = tm // c_len
    tot = _cat0(*[jnp.broadcast_to(cum[(c + 1) * c_len - 1:(c + 1) * c_len, :], (c_len, RW_WIDTH))
                  for c in range(nchunks)])
    e_neg = jnp.exp(-cum)
    e_rest = jnp.exp(tot - cum)
    at = -kk * jnp.exp(cum - lw)
    bt = b * e_neg
    kt = k_h * e_neg
    rt = r * jnp.exp(cum)
    bh = b * e_rest
    kh = k_h * e_rest
    ptot = jnp.exp(tot)

    lane = lax.broadcasted_iota(jnp.int32, (c_len, LANES), 1)
    first = lane < half
    rr = lax.broadcasted_iota(jnp.int32, (LANES, LANES), 0)
    cc = lax.broadcasted_iota(jnp.int32, (LANES, LANES), 1)
    strict = rr > cc
    incl = rr >= cc
    eye = (rr == cc).astype(F32)
    zeros_ll = jnp.zeros((LANES, LANES), F32)
    steps = int(math.log2(c_len))

    def stack(x):
        return _cat0(jnp.where(first, x, 0.0), jnp.where(first, 0.0, x))

    s_cur = [s_ref[q] for q in range(RW_PAIRS)]
    ys = []
    group = min(RW_GROUP, nchunks)
    for g0 in range(0, nchunks, group):
        units = [(c, q) for c in range(g0, g0 + group) for q in range(RW_PAIRS)]
        every = range(len(units))

        def stacks(x):
            return [stack(x[c * c_len:(c + 1) * c_len, q * LANES:(q + 1) * LANES]) for c, q in units]

        la, lr, sb, sk, sv, sbh, skh = (stacks(x) for x in (at, rt, bt, kt, v, bh, kh))
        prod = [_mm(_cat0(la[u], lr[u]), _cat0(sb[u], sk[u]), NT_DIMS) for u in every]
        a_ab = [jnp.where(strict, prod[u][:LANES, :LANES], 0.0) for u in every]
        a_ak = [jnp.where(strict, prod[u][:LANES, LANES:], 0.0) for u in every]
        a_rb = [jnp.where(incl, prod[u][LANES:, :LANES], 0.0) for u in every]
        a_rk = [jnp.where(incl, prod[u][LANES:, LANES:], 0.0) for u in every]
        av = [_mm(a_ak[u], sv[u]) for u in every]
        pw = [_mm(a_ab[u], a_ab[u]) for u in every]
        inv = [eye + a_ab[u] for u in every]
        for m in range(1, steps):
            if m == steps - 1:
                res = [_mm(pw[u], inv[u]) for u in every]
                inv = [inv[u] + res[u] for u in every]
            else:
                res = [_mm(pw[u], _cat1(pw[u], inv[u])) for u in every]
                pw = [res[u][:, :LANES] for u in every]
                inv = [inv[u] + res[u][:, LANES:] for u in every]
        wmat = [_mm(inv[u], _cat1(la[u], av[u])) for u in every]
        z = [_cat0(wmat[u], _cat1(zeros_ll, sv[u])) for u in every]
        qy = [_mm(_cat1(a_rb[u], a_rk[u]), z[u]) for u in every]
        gn = [_mm(z[u], _cat0(sbh[u], skh[u]), TN_DIMS) for u in every]
        qt = [lr[u] + qy[u][:, :LANES] for u in every]
        qt = [qt[u][:c_len] + qt[u][c_len:] for u in every]
        y0 = [qy[u][:c_len, LANES:] + qy[u][c_len:, LANES:] for u in every]
        for ci in range(group):
            c = g0 + ci
            us = [ci * RW_PAIRS + q for q in range(RW_PAIRS)]
            y_pairs = [y0[u] + _mm(qt[u], s_cur[q], NT_DIMS) for q, u in enumerate(us)]
            upd = [_mm(s_cur[q], gn[u][:LANES]) for q, u in enumerate(us)]
            s_cur = [s_cur[q] * ptot[c * c_len:c * c_len + 1, q * LANES:(q + 1) * LANES] + upd[q] + gn[u][LANES:]
                     for q, u in enumerate(us)]
            ys.append(_cat1(*y_pairs))
    for q in range(RW_PAIRS):
        s_ref[q] = s_cur[q]
    y = _cat0(*ys) if nchunks > 1 else ys[0]

    inv_n = 1.0 / RW_HEAD_DIM
    mean = _pair_sums(y, ones_pair) * inv_n
    d = y - mean
    var = _pair_sums(d * d, ones_pair) * inv_n
    y = d * lax.rsqrt(var + RW_LN_EPS) * lnw_ref[...] + lnb_ref[...]
    out = (y + bonus) * g
    o_ref[...] = out[:valid]

    @pl.when(i == pl.num_programs(1) - 1)
    def _():
        for q in range(RW_PAIRS):
            sout_ref[0, 2 * q] = s_cur[q][:half, :half]
            sout_ref[0, 2 * q + 1] = s_cur[q][half:, half:]


def _rwkv_mixer(p_rw, prev_rows, s0, batch, seq, lp):
    n = batch * seq
    if seq % RW_CHUNK == 0:
        tm = valid = min(256, seq)
    else:
        assert seq < RW_CHUNK and seq % 8 == 0
        tm, valid = RW_CHUNK, seq
    nt = seq // valid
    idx = np.arange(tm)
    tril = (((idx[:, None] // RW_CHUNK) == (idx[None, :] // RW_CHUNK)) & (idx[None, :] <= idx[:, None]))
    lane = np.arange(LANES) // RW_HEAD_DIM
    ones_pair = lane[:, None] == lane[None, :]
    vec = lambda width: pl.BlockSpec((1, width), lambda b, i: (0, 0))
    full = lambda r, c: pl.BlockSpec((r, c), lambda b, i: (0, 0))
    state = pl.BlockSpec((1, RW_HEADS, RW_HEAD_DIM, RW_HEAD_DIM), lambda b, i: (b, 0, 0, 0))
    return pl.pallas_call(
        functools.partial(_rwkv_body, tm=tm, valid=valid),
        grid=(batch, nt),
        in_specs=[pl.BlockSpec((valid, RW_PROJ), lambda b, i: (b * nt + i, 0)),
                  pl.BlockSpec((1, 1, RW_PROJ), lambda b, i: (b, 0, 0)),
                  state,
                  vec(RW_PROJ), vec(RW_WIDTH), full(RW_DECAY_LORA, RW_WIDTH),
                  vec(RW_WIDTH), full(RW_AAA_LORA, RW_WIDTH), full(RW_GATE_LORA, RW_WIDTH),
                  vec(RW_WIDTH), vec(RW_WIDTH), vec(RW_WIDTH), vec(RW_WIDTH), vec(RW_WIDTH),
                  full(tm, tm), full(LANES, LANES)],
        out_specs=[pl.BlockSpec((valid, RW_WIDTH), lambda b, i: (b * nt + i, 0)), state],
        out_shape=[jax.ShapeDtypeStruct((n, RW_WIDTH), F32),
                   jax.ShapeDtypeStruct((batch, RW_HEADS, RW_HEAD_DIM, RW_HEAD_DIM), F32)],
        scratch_shapes=[pltpu.VMEM((1, RW_PROJ), F32),
                        pltpu.VMEM((RW_PAIRS, LANES, LANES), F32)],
        compiler_params=_params("parallel", "arbitrary"),
        name="rwkv",
    )(p_rw, prev_rows.reshape(batch, 1, RW_PROJ), s0,
      lp["mu"].reshape(1, RW_PROJ), lp["w0"].reshape(1, RW_WIDTH), lp["w2"],
      lp["a0"].reshape(1, RW_WIDTH), lp["a2"], lp["g2"],
      lp["k_k"].reshape(1, RW_WIDTH), lp["k_a"].reshape(1, RW_WIDTH), lp["r_k"].reshape(1, RW_WIDTH),
      lp["lnx_w"].reshape(1, RW_WIDTH), lp["lnx_b"].reshape(1, RW_WIDTH),
      jnp.asarray(tril, BF16), jnp.asarray(ones_pair, BF16))


def _lam_of(lam_ref, lam_init):
    lq1, lk1, lq2, lk2 = lam_ref[0:1, :], lam_ref[1:2, :], lam_ref[2:3, :], lam_ref[3:4, :]
    return (jnp.exp(jnp.sum(lq1 * lk1, axis=-1, keepdims=True))
            - jnp.exp(jnp.sum(lq2 * lk2, axis=-1, keepdims=True)) + lam_init)


def _diff_finish(acc1, l1, acc2, l2, lam, subln_w, lam_init):
    o = acc1 / l1 - lam * (acc2 / l2)
    o = o * lax.rsqrt(jnp.mean(o * o, axis=-1, keepdims=True) + SUBLN_EPS) * subln_w
    return o * (1.0 - lam_init)


def _attn_prompt_body(qi_ref, kj_ref, slopes_ref, q_ref, k_ref, vt_ref, lam_ref, subln_ref, o_ref,
                      m_ref, l_ref, acc_ref, bias_ref, *, lam_init, tq, tk):
    h = pl.program_id(1)
    t = pl.program_id(2)
    i = qi_ref[t]
    j = kj_ref[t]
    neg_slope = -slopes_ref[h] * LOG2E

    @pl.when(t == 0)
    def _():
        kpos = lax.broadcasted_iota(jnp.int32, (tk, tq), 0)
        qpos = lax.broadcasted_iota(jnp.int32, (tk, tq), 1)
        bias_ref[...] = (qpos - kpos).astype(F32) * neg_slope

    @pl.when(j == 0)
    def _():
        m_ref[...] = jnp.full(m_ref.shape, NEG_BIG, F32)
        l_ref[...] = jnp.zeros(l_ref.shape, F32)
        acc_ref[...] = jnp.zeros(acc_ref.shape, F32)

    off = jnp.full((1, 1), i * tq - j * tk, jnp.int32).astype(F32) * neg_slope

    def update(on_diagonal):
        q = q_ref[...]
        k = k_ref[...]
        vt = vt_ref[...]
        units = [(c, q0) for c in range(2) for q0 in range(0, tq, ATTN_Q_BLOCK)]
        every = range(len(units))
        ss = []
        for c, q0 in units:
            sl = slice(c * DA_HEAD_DIM, (c + 1) * DA_HEAD_DIM)
            qs = slice(q0, q0 + ATTN_Q_BLOCK)
            s = lax.dot_general(k[:, sl], q[qs, sl], NT_DIMS, preferred_element_type=F32) + bias_ref[:, qs]
            if on_diagonal:
                kpos = lax.broadcasted_iota(jnp.int32, (tk, ATTN_Q_BLOCK), 0)
                qpos = lax.broadcasted_iota(jnp.int32, (tk, ATTN_Q_BLOCK), 1) + q0
                s = jnp.where(qpos >= kpos, s, NEG_BIG)
            ss.append(s)
        m_prev = [m_ref[c, :, q0:q0 + ATTN_Q_BLOCK] for c, q0 in units]
        m_new = [jnp.maximum(m_prev[u], jnp.max(ss[u], axis=0, keepdims=True) + off) for u in every]
        ps = [jnp.exp2(ss[u] - (m_new[u] - off)) for u in every]
        pv = [jnp.dot(vt, ps[u].astype(BF16), preferred_element_type=F32) for u in every]
        for u, (c, q0) in enumerate(units):
            qs = slice(q0, q0 + ATTN_Q_BLOCK)
            alpha = jnp.exp2(m_prev[u] - m_new[u])
            l_ref[c, :, qs] = alpha * l_ref[c, :, qs] + jnp.sum(ps[u], axis=0, keepdims=True)
            acc_ref[c, :, qs] = alpha * acc_ref[c, :, qs] + pv[u]
            m_ref[c, :, qs] = m_new[u]

    @pl.when(j < i)
    def _():
        update(False)

    @pl.when(j == i)
    def _():
        update(True)
        lam = _lam_of(lam_ref, lam_init)
        o = acc_ref[0] / l_ref[0] - lam * (acc_ref[1] / l_ref[1])
        o = o * lax.rsqrt(jnp.mean(o * o, axis=0, keepdims=True) + SUBLN_EPS)
        o_ref[...] = o.T * subln_ref[...] * (1.0 - lam_init)


def _alibi_slopes():
    return jnp.exp2(-ALIBI_MAX_BIAS / DA_HEADS * jnp.arange(1, DA_HEADS + 1, dtype=F32))


ATTN_Q_BLOCK = 256


def _attn_prompt(q, k, vt, lam_params, subln_w, lam_init, batch, seq):
    n = batch * seq
    tq = tk = min(512, seq)
    assert tq % ATTN_Q_BLOCK == 0
    nq = seq // tq
    pairs = [(i, j) for i in range(nq) for j in range(i + 1)]
    qi = jnp.asarray([pr[0] for pr in pairs], jnp.int32)
    kj = jnp.asarray([pr[1] for pr in pairs], jnp.int32)
    body = functools.partial(_attn_prompt_body, lam_init=lam_init, tq=tq, tk=tk)
    grid_spec = pltpu.PrefetchScalarGridSpec(
        num_scalar_prefetch=3,
        grid=(batch, DA_HEADS, len(pairs)),
        in_specs=[pl.BlockSpec((tq, DA_V_DIM), lambda b, h, t, qi, kj, sl: (b * nq + qi[t], h)),
                  pl.BlockSpec((tk, DA_V_DIM), lambda b, h, t, qi, kj, sl: (b * nq + kj[t], h)),
                  pl.BlockSpec((DA_V_DIM, tk), lambda b, h, t, qi, kj, sl: (h, b * nq + kj[t])),
                  pl.BlockSpec((4, DA_HEAD_DIM), lambda b, h, t, qi, kj, sl: (0, 0)),
                  pl.BlockSpec((1, DA_V_DIM), lambda b, h, t, qi, kj, sl: (0, 0))],
        out_specs=pl.BlockSpec((tq, DA_V_DIM), lambda b, h, t, qi, kj, sl: (b * nq + qi[t], h)),
        scratch_shapes=[pltpu.VMEM((2, 1, tq), F32), pltpu.VMEM((2, 1, tq), F32),
                        pltpu.VMEM((2, DA_V_DIM, tq), F32), pltpu.VMEM((tk, tq), F32)],
    )
    return pl.pallas_call(
        body,
        grid_spec=grid_spec,
        out_shape=jax.ShapeDtypeStruct((n, DA_WIDTH), F32),
        compiler_params=_params("parallel", "parallel", "arbitrary"),
        name="attn_prompt",
    )(qi, kj, _alibi_slopes(), q, k, vt, lam_params, subln_w.reshape(1, DA_V_DIM))


def _attn_sample_body(pt_ref, slopes_ref, q_ref, kn_ref, vn_ref, lam_ref, subln_ref, *rest,
                      lam_init, pages_per_step, page_size, n_past, n_new):
    kp_refs = rest[:pages_per_step]
    vp_refs = rest[pages_per_step:2 * pages_per_step]
    o_ref = rest[2 * pages_per_step]
    qbd_ref, m_ref, l_ref, acc_ref = rest[2 * pages_per_step + 1:]
    del pt_ref
    j = pl.program_id(1)
    rows2 = 2 * n_new
    scale = DA_HEAD_DIM ** -0.5

    @pl.when(j == 0)
    def _():
        q = q_ref[...] * scale
        lane = lax.broadcasted_iota(jnp.int32, (n_new, DA_V_DIM), 1)
        for h in range(DA_HEADS):
            qh = q[:, h * DA_V_DIM:(h + 1) * DA_V_DIM]
            top = jnp.where(lane < DA_HEAD_DIM, qh, 0.0)
            bot = jnp.where(lane >= DA_HEAD_DIM, qh, 0.0)
            qbd_ref[h] = jnp.concatenate([top, bot], axis=0)
        m_ref[...] = jnp.full(m_ref.shape, NEG_BIG, F32)
        l_ref[...] = jnp.zeros(l_ref.shape, F32)
        acc_ref[...] = jnp.zeros(acc_ref.shape, F32)

    qpos = n_past + (lax.broadcasted_iota(jnp.int32, (rows2, page_size), 0) % n_new)
    kpos0 = lax.broadcasted_iota(jnp.int32, (rows2, page_size), 1) + j * (pages_per_step * page_size)
    heads = range(DA_HEADS)
    pages = range(pages_per_step)
    head_rows = [pl.ds(h, page_size, stride=DA_HEADS) for h in heads]
    dists = [(qpos - (kpos0 + u * page_size)).astype(F32) for u in pages]
    qbd = [qbd_ref[h].astype(BF16) for h in heads]
    ss = [[lax.dot_general(qbd[h], kp_refs[u][0, 0, head_rows[h], :].astype(BF16), NT_DIMS,
                           preferred_element_type=F32) - slopes_ref[h] * dists[u]
           for u in pages] for h in heads]
    m_prev = [m_ref[h] for h in heads]
    m_new = [jnp.maximum(m_prev[h], jnp.max(functools.reduce(jnp.maximum, ss[h]), axis=-1, keepdims=True))
             for h in heads]
    ps = [[jnp.exp(ss[h][u] - m_new[h]) for u in pages] for h in heads]
    pv = [None] * DA_HEADS
    for u in pages:
        for h in heads:
            d = jnp.dot(ps[h][u].astype(BF16), vp_refs[u][0, 0, head_rows[h], :].astype(BF16),
                        preferred_element_type=F32)
            pv[h] = d if pv[h] is None else pv[h] + d
    for h in heads:
        alpha = jnp.exp(m_prev[h] - m_new[h])
        l_ref[h] = alpha * l_ref[h] + jnp.sum(functools.reduce(jnp.add, ps[h]), axis=-1, keepdims=True)
        acc_ref[h] = alpha * acc_ref[h] + pv[h]
        m_ref[h] = m_new[h]

    @pl.when(j == pl.num_programs(1) - 1)
    def _():
        lam = _lam_of(lam_ref, lam_init)
        qn = lax.broadcasted_iota(jnp.int32, (rows2, n_new), 0) % n_new
        kn = lax.broadcasted_iota(jnp.int32, (rows2, n_new), 1)
        dist_new = qn - kn
        for h in range(DA_HEADS):
            hs = slice(h * DA_V_DIM, (h + 1) * DA_V_DIM)
            s = lax.dot_general(qbd_ref[h], kn_ref[:, hs], NT_DIMS, preferred_element_type=F32)
            s = s - slopes_ref[h] * dist_new.astype(F32)
            s = jnp.where(dist_new >= 0, s, NEG_BIG)
            m_prev = m_ref[h]
            m_new = jnp.maximum(m_prev, jnp.max(s, axis=-1, keepdims=True))
            alpha = jnp.exp(m_prev - m_new)
            p = jnp.exp(s - m_new)
            l = alpha * l_ref[h] + jnp.sum(p, axis=-1, keepdims=True)
            acc = alpha * acc_ref[h] + jnp.dot(p, vn_ref[:, hs], preferred_element_type=F32)
            o_ref[:, hs] = _diff_finish(acc[:n_new], l[:n_new], acc[n_new:], l[n_new:], lam,
                                        subln_ref[...], lam_init)


def _attn_sample(q, k_new, v_new, cache_k, cache_v, page_table, lam_params, subln_w, lam_init, layer,
                 dec_batch, dec_seq):
    page_size = cache_k.shape[2] // DA_HEADS
    n_pages = page_table.shape[1]
    pps = 8 if n_pages % 8 == 0 else 1
    n_past = n_pages * page_size
    body = functools.partial(_attn_sample_body, lam_init=lam_init, pages_per_step=pps,
                             page_size=page_size, n_past=n_past, n_new=dec_seq)
    tok = pl.BlockSpec((dec_seq, DA_QK), lambda b, j, pt, sl: (b, 0))

    def page_spec(u):
        return pl.BlockSpec((1, 1, page_size * DA_HEADS, DA_V_DIM),
                            lambda b, j, pt, sl: (layer, pt[b, j * pps + u], 0, 0))

    grid_spec = pltpu.PrefetchScalarGridSpec(
        num_scalar_prefetch=2,
        grid=(dec_batch, n_pages // pps),
        in_specs=[tok, tok, tok,
                  pl.BlockSpec((4, DA_HEAD_DIM), lambda b, j, pt, sl: (0, 0)),
                  pl.BlockSpec((1, DA_V_DIM), lambda b, j, pt, sl: (0, 0))]
                 + [page_spec(u) for u in range(pps)] + [page_spec(u) for u in range(pps)],
        out_specs=tok,
        scratch_shapes=[pltpu.VMEM((DA_HEADS, 2 * dec_seq, DA_V_DIM), F32),
                        pltpu.VMEM((DA_HEADS, 2 * dec_seq, 1), F32),
                        pltpu.VMEM((DA_HEADS, 2 * dec_seq, 1), F32),
                        pltpu.VMEM((DA_HEADS, 2 * dec_seq, DA_V_DIM), F32)],
    )
    return pl.pallas_call(
        body,
        grid_spec=grid_spec,
        out_shape=jax.ShapeDtypeStruct((dec_batch * dec_seq, DA_WIDTH), F32),
        compiler_params=_params("parallel", "arbitrary"),
        name="attn_sample",
    )(page_table, _alibi_slopes(), q, k_new, v_new, lam_params, subln_w.reshape(1, DA_V_DIM),
      *([cache_k] * pps), *([cache_v] * pps))


FFN_CHUNK = 256


def _ffn_body(x_ref, orw_ref, oda_ref, wo_ref, g2_ref, w1_ref, w3_ref, w2_ref, fg_ref, out_ref, *, final):
    x1 = (x_ref[...]
          + jnp.dot(orw_ref[...].astype(BF16), wo_ref[:RW_WIDTH, :], preferred_element_type=F32)
          + jnp.dot(oda_ref[...].astype(BF16), wo_ref[RW_WIDTH:, :], preferred_element_type=F32))
    hf = _rms(x1, g2_ref[...], NORM_EPS).astype(BF16)
    acc = jnp.zeros_like(x1)
    for c0 in range(0, D_FF, FFN_CHUNK):
        a = jnp.dot(hf, w1_ref[:, c0:c0 + FFN_CHUNK], preferred_element_type=F32)
        b = jnp.dot(hf, w3_ref[:, c0:c0 + FFN_CHUNK], preferred_element_type=F32)
        z = (a * jax.nn.sigmoid(a) * b).astype(BF16)
        acc = acc + jnp.dot(z, w2_ref[c0:c0 + FFN_CHUNK, :], preferred_element_type=F32)
    x2 = x1 + acc
    out_ref[...] = _rms(x2, fg_ref[...], NORM_EPS) if final else x2


def _outproj_ffn(x2d, o_rw, o_da, wo, g2, w1, w3, w2, final_g, final):
    n = x2d.shape[0]
    tm = min(512, n)
    row = lambda width: pl.BlockSpec((tm, width), lambda i: (i, 0))
    full = lambda r, c: pl.BlockSpec((r, c), lambda i: (0, 0))
    return pl.pallas_call(
        functools.partial(_ffn_body, final=final),
        grid=(n // tm,),
        in_specs=[row(D_MODEL), row(RW_WIDTH), row(DA_WIDTH), full(D_MODEL, D_MODEL), full(1, D_MODEL),
                  full(D_MODEL, D_FF), full(D_MODEL, D_FF), full(D_FF, D_MODEL), full(1, D_MODEL)],
        out_specs=row(D_MODEL),
        out_shape=jax.ShapeDtypeStruct((n, D_MODEL), F32),
        compiler_params=_params("parallel"),
        name="outproj_ffn",
    )(x2d, o_rw, o_da, wo, g2.reshape(1, D_MODEL), w1, w3, w2, final_g.reshape(1, D_MODEL))


def _lambda_init(layer):
    return 0.8 - 0.6 * math.exp(-0.3 * layer)


def kernel(x_prompt, x_sample, cache_k, cache_v, state_wkv, state_shift, page_table, norm1_g, w_in, rw_mu, rw_w0, rw_w2, rw_a0, rw_a2, rw_g2, rw_k_k, rw_k_a, rw_r_k, rw_lnx_w, rw_lnx_b, da_lam_q1, da_lam_k1, da_lam_q2, da_lam_k2, da_subln_w, w_out, norm2_g, ffn_w1, ffn_w3, ffn_w2, final_g):
    depth = w_in.shape[0]
    batch, seq, _ = x_prompt.shape
    dec_batch, dec_seq, _ = x_sample.shape
    n_pool, page_size = cache_k.shape[1], cache_k.shape[2]
    ck = cache_k.reshape(depth, n_pool, page_size * DA_HEADS, 2 * DA_HEAD_DIM)
    cv = cache_v.reshape(depth, n_pool, page_size * DA_HEADS, DA_V_DIM)

    xp = x_prompt.reshape(batch * seq, D_MODEL)
    xs = x_sample.reshape(dec_batch * dec_seq, D_MODEL)
    outs = {name: [] for name in ("kp", "vp", "sp", "shp", "ks", "vs", "ss", "shs")}
    for l in range(depth):
        lam_init = _lambda_init(l)
        lp = dict(mu=rw_mu[l], w0=rw_w0[l], w2=rw_w2[l], a0=rw_a0[l], a2=rw_a2[l], g2=rw_g2[l],
                  k_k=rw_k_k[l], k_a=rw_k_a[l], r_k=rw_r_k[l].reshape(RW_WIDTH),
                  lnx_w=rw_lnx_w[l], lnx_b=rw_lnx_b[l])
        lam_params = jnp.stack([da_lam_q1[l], da_lam_k1[l], da_lam_q2[l], da_lam_k2[l]])
        w_in_b = w_in[l].astype(BF16)
        wo_b, w1_b, w3_b, w2_b = (w_out[l].astype(BF16), ffn_w1[l].astype(BF16),
                                  ffn_w3[l].astype(BF16), ffn_w2[l].astype(BF16))
        final = l == depth - 1

        p_rw, k, v, q_b, k_b, vt_b = _inproj_prompt(xp, norm1_g[l], w_in_b)
        o_rw, s_new = _rwkv_mixer(p_rw, jnp.zeros((batch, RW_PROJ), F32),
                                  jnp.zeros((batch, RW_HEADS, RW_HEAD_DIM, RW_HEAD_DIM), F32),
                                  batch, seq, lp)
        o_da = _attn_prompt(q_b, k_b, vt_b, lam_params, da_subln_w[l], lam_init, batch, seq)
        xp = _outproj_ffn(xp, o_rw, o_da, wo_b, norm2_g[l], w1_b, w3_b, w2_b, final_g, final)
        outs["kp"].append(k.reshape(batch, seq, DA_HEADS, 2 * DA_HEAD_DIM))
        outs["vp"].append(v.reshape(batch, seq, DA_HEADS, DA_V_DIM))
        outs["sp"].append(s_new)
        outs["shp"].append(p_rw.reshape(batch, seq, RW_PROJ)[:, -1])

        p_rw, q, k, v = _inproj(xs, norm1_g[l], w_in_b)
        o_rw, s_new = _rwkv_mixer(p_rw, state_shift[l], state_wkv[l], dec_batch, dec_seq, lp)
        o_da = _attn_sample(q, k, v, ck, cv, page_table, lam_params, da_subln_w[l], lam_init, l,
                            dec_batch, dec_seq)
        xs = _outproj_ffn(xs, o_rw, o_da, wo_b, norm2_g[l], w1_b, w3_b, w2_b, final_g, final)
        outs["ks"].append(k.reshape(dec_batch, dec_seq, DA_HEADS, 2 * DA_HEAD_DIM))
        outs["vs"].append(v.reshape(dec_batch, dec_seq, DA_HEADS, DA_V_DIM))
        outs["ss"].append(s_new)
        outs["shs"].append(p_rw.reshape(dec_batch, dec_seq, RW_PROJ)[:, -1])

    return (xp.reshape(batch, seq, D_MODEL), xs.reshape(dec_batch, dec_seq, D_MODEL),
            jnp.stack(outs["kp"]), jnp.stack(outs["vp"]), jnp.stack(outs["sp"]), jnp.stack(outs["shp"]),
            jnp.stack(outs["ks"]), jnp.stack(outs["vs"]), jnp.stack(outs["ss"]), jnp.stack(outs["shs"]))
```

```python
import functools
import math

import jax
import jax.numpy as jnp
import numpy as np
from jax import lax
from jax.experimental import pallas as pl
from jax.experimental.pallas import tpu as pltpu

F32 = jnp.float32
BF16 = jnp.bfloat16

D_MODEL = 1024
RW_HEAD_DIM = 64
RW_WIDTH = 512
RW_HEADS = RW_WIDTH // RW_HEAD_DIM
RW_DECAY_LORA = 64
RW_AAA_LORA = 64
RW_GATE_LORA = 128
RW_LN_EPS = 64e-5
RW_PROJ = 3 * RW_WIDTH + RW_DECAY_LORA + RW_AAA_LORA + RW_GATE_LORA
DA_HEAD_DIM = 64
DA_V_DIM = 2 * DA_HEAD_DIM
DA_HEADS = 4
DA_QK = DA_HEADS * 2 * DA_HEAD_DIM
DA_WIDTH = DA_HEADS * DA_V_DIM
ALIBI_MAX_BIAS = 8.0
D_FF = 2816
NORM_EPS = 1e-5
SUBLN_EPS = 1e-5
NEG_BIG = -1e30
LOG2E = 1.4426950408889634

LANES = 128
RW_PAIRS = RW_WIDTH // LANES
RW_CHUNK = 64
RW_GROUP = 2

VMEM_LIMIT_BYTES = 56 * 1024 * 1024

NT_DIMS = (((1,), (1,)), ((), ()))
TN_DIMS = (((0,), (0,)), ((), ()))


def _params(*semantics):
    return pltpu.CompilerParams(dimension_semantics=semantics,
                                vmem_limit_bytes=VMEM_LIMIT_BYTES)


def _rms(x, g, eps):
    return x * lax.rsqrt(jnp.mean(x * x, axis=-1, keepdims=True) + eps) * g


def _mm(a, b, dims=None):
    a, b = a.astype(BF16), b.astype(BF16)
    if dims is None:
        return jnp.dot(a, b, preferred_element_type=F32)
    return lax.dot_general(a, b, dims, preferred_element_type=F32)


def _split_terms(x, terms):
    out = []
    for _ in range(terms):
        hi = x.astype(BF16)
        out.append(hi)
        x = x - hi.astype(F32)
    return out


def _inproj_body(x_ref, g_ref, w_ref, prw_ref, q_ref, k_ref, v_ref):
    h = _rms(x_ref[...], g_ref[...], NORM_EPS).astype(BF16)
    c0, c1, c2 = RW_PROJ, RW_PROJ + DA_QK, RW_PROJ + 2 * DA_QK
    prw_ref[...] = jnp.dot(h, w_ref[:, :c0], preferred_element_type=F32)
    q_ref[...] = jnp.dot(h, w_ref[:, c0:c1], preferred_element_type=F32)
    k_ref[...] = jnp.dot(h, w_ref[:, c1:c2], preferred_element_type=F32)
    v_ref[...] = jnp.dot(h, w_ref[:, c2:], preferred_element_type=F32)


def _inproj_prompt_body(x_ref, g_ref, w_ref, wvt_ref, *rest, n_prev):
    prev = rest[:2 * n_prev]
    prw_ref, k_ref, v_ref, qb_ref, kb_ref, vtb_ref = rest[2 * n_prev:]
    if n_prev:
        for li in range(n_prev):
            k_ref[li] = prev[2 * li][...]
            v_ref[li] = prev[2 * li + 1][...]
        k_ref, v_ref = k_ref.at[n_prev], v_ref.at[n_prev]
    tm = x_ref.shape[0]
    h = _rms(x_ref[...], g_ref[...], NORM_EPS).astype(BF16)
    c0, c1, c2 = RW_PROJ, RW_PROJ + DA_QK, RW_PROJ + 2 * DA_QK
    prw_ref[...] = jnp.dot(h, w_ref[:, :c0], preferred_element_type=F32)
    q = jnp.dot(h, w_ref[:, c0:c1], preferred_element_type=F32)
    qb_ref[...] = (q * (DA_HEAD_DIM ** -0.5 * LOG2E)).astype(BF16)
    k = jnp.dot(h, w_ref[:, c1:c2], preferred_element_type=F32)
    kb_ref[...] = k.astype(BF16)
    v = jnp.dot(h, w_ref[:, c2:], preferred_element_type=F32)
    for hd in range(DA_HEADS):
        head_rows = pl.ds(hd, tm, stride=DA_HEADS)
        k_ref[head_rows, :] = k[:, hd * DA_V_DIM:(hd + 1) * DA_V_DIM]
        v_ref[head_rows, :] = v[:, hd * DA_V_DIM:(hd + 1) * DA_V_DIM]
    vtb_ref[...] = lax.dot_general(wvt_ref[...], h, NT_DIMS, preferred_element_type=F32).astype(BF16)


def _inproj_prompt(x2d, g, w_bf16, kv_prev):
    n = x2d.shape[0]
    tm = min(512, n)
    in_proj = w_bf16.shape[1]
    wvt = w_bf16[:, RW_PROJ + 2 * DA_QK:].T
    n_prev = len(kv_prev)
    row = lambda width: pl.BlockSpec((tm, width), lambda i: (i, 0))
    cache_rows = pl.BlockSpec((tm * DA_HEADS, DA_V_DIM), lambda i: (i, 0))
    if n_prev:
        cache_out = pl.BlockSpec((n_prev + 1, tm * DA_HEADS, DA_V_DIM), lambda i: (0, i, 0))
        cache_shape = jax.ShapeDtypeStruct((n_prev + 1, n * DA_HEADS, DA_V_DIM), F32)
    else:
        cache_out = cache_rows
        cache_shape = jax.ShapeDtypeStruct((n * DA_HEADS, DA_V_DIM), F32)
    return pl.pallas_call(
        functools.partial(_inproj_prompt_body, n_prev=n_prev),
        grid=(n // tm,),
        in_specs=[row(D_MODEL),
                  pl.BlockSpec((1, D_MODEL), lambda i: (0, 0)),
                  pl.BlockSpec((D_MODEL, in_proj), lambda i: (0, 0)),
                  pl.BlockSpec((DA_WIDTH, D_MODEL), lambda i: (0, 0))] + [cache_rows] * (2 * n_prev),
        out_specs=[row(RW_PROJ), cache_out, cache_out, row(DA_QK), row(DA_QK),
                   pl.BlockSpec((DA_WIDTH, tm), lambda i: (0, i))],
        out_shape=[jax.ShapeDtypeStruct((n, RW_PROJ), F32), cache_shape, cache_shape,
                   jax.ShapeDtypeStruct((n, DA_QK), BF16),
                   jax.ShapeDtypeStruct((n, DA_QK), BF16),
                   jax.ShapeDtypeStruct((DA_WIDTH, n), BF16)],
        compiler_params=_params("parallel"),
        name="inproj_prompt",
    )(x2d, g.reshape(1, D_MODEL), w_bf16, wvt, *[a for kv in kv_prev for a in kv])


def _inproj(x2d, g, w_bf16):
    n = x2d.shape[0]
    tm = min(512, n)
    in_proj = w_bf16.shape[1]
    row = lambda width: pl.BlockSpec((tm, width), lambda i: (i, 0))
    return pl.pallas_call(
        _inproj_body,
        grid=(n // tm,),
        in_specs=[row(D_MODEL),
                  pl.BlockSpec((1, D_MODEL), lambda i: (0, 0)),
                  pl.BlockSpec((D_MODEL, in_proj), lambda i: (0, 0))],
        out_specs=[row(RW_PROJ), row(DA_QK), row(DA_QK), row(DA_WIDTH)],
        out_shape=[jax.ShapeDtypeStruct((n, RW_PROJ), F32),
                   jax.ShapeDtypeStruct((n, DA_QK), F32),
                   jax.ShapeDtypeStruct((n, DA_QK), F32),
                   jax.ShapeDtypeStruct((n, DA_WIDTH), F32)],
        compiler_params=_params("parallel"),
        name="inproj",
    )(x2d, g.reshape(1, D_MODEL), w_bf16)


def _softplus(z):
    return jnp.maximum(z, 0.0) + jnp.log(1.0 + jnp.exp(-jnp.abs(z)))


def _pair_sums(x, ones_pair):
    cols = []
    for q in range(RW_PAIRS):
        terms = _split_terms(x[:, q * LANES:(q + 1) * LANES], 2)
        cols.append(sum(jnp.dot(t, ones_pair, preferred_element_type=F32) for t in terms))
    return jnp.concatenate(cols, axis=-1)


def _cat0(*xs):
    return jnp.concatenate(xs, axis=0)


def _cat1(*xs):
    return jnp.concatenate(xs, axis=1)


def _rwkv_body(p_ref, prev_ref, s0_ref, mu_ref, w0_ref, w2_ref, a0_ref, a2_ref, g2_ref,
               kk_ref, ka_ref, rk_ref, lnw_ref, lnb_ref, tril_ref, ones_ref,
               o_ref, sout_ref, carry_ref, s_ref, *, tm, valid):
    i = pl.program_id(1)
    c_len = RW_CHUNK
    half = RW_HEAD_DIM
    zeros_hh = jnp.zeros((half, half), F32)

    @pl.when(i == 0)
    def _():
        carry_ref[...] = prev_ref[0]
        for q in range(RW_PAIRS):
            top = _cat1(s0_ref[0, 2 * q], zeros_hh)
            bot = _cat1(zeros_hh, s0_ref[0, 2 * q + 1])
            s_ref[q] = _cat0(top, bot)

    p = p_ref[...]
    prev_row = carry_ref[...]
    carry_ref[...] = p[valid - 1:valid, :]
    if valid < tm:
        p = _cat0(p, jnp.zeros((tm - valid, RW_PROJ), F32))
    rolled = pltpu.roll(p, 1, axis=0)
    row = lax.broadcasted_iota(jnp.int32, (tm, 1), 0)
    prev = jnp.where(row == 0, prev_row, rolled)
    xs = p + (prev - p) * mu_ref[...]

    c1, c2, c3 = RW_WIDTH, 2 * RW_WIDTH, 3 * RW_WIDTH
    c4 = c3 + RW_DECAY_LORA
    c5 = c4 + RW_AAA_LORA
    r, k, v = xs[:, :c1], xs[:, c1:c2], xs[:, c2:c3]
    wd, ad, gd = xs[:, c3:c4], xs[:, c4:c5], xs[:, c5:]

    w = -_softplus(-(w0_ref[...] + jnp.dot(jnp.tanh(wd), w2_ref[...], preferred_element_type=F32))) - 0.5
    lw = -jnp.exp(w)
    a = jax.nn.sigmoid(a0_ref[...] + jnp.dot(ad, a2_ref[...], preferred_element_type=F32))
    g = jnp.dot(jax.nn.sigmoid(gd), g2_ref[...], preferred_element_type=F32)

    ones_pair = ones_ref[...]
    kk = k * kk_ref[...]
    kk = kk / jnp.maximum(jnp.sqrt(_pair_sums(kk * kk, ones_pair)), 1e-12)
    k_h = k * (1.0 + (a - 1.0) * ka_ref[...])
    bonus = _pair_sums(r * k_h * rk_ref[...], ones_pair) * v
    b = kk * a

    if valid < tm:
        live = row < valid
        lw = jnp.where(live, lw, 0.0)
        kk, b, k_h, r, v = (jnp.where(live, t, 0.0) for t in (kk, b, k_h, r, v))

    tril = tril_ref[...]
    cum = sum(jnp.dot(tril, t, preferred_element_type=F32) for t in _split_terms(lw, 3))
    nchunks = tm // c_len
    tot = _cat0(*[jnp.broadcast_to(cum[(c + 1) * c_len - 1:(c + 1) * c_len, :], (c_len, RW_WIDTH))
                  for c in range(nchunks)])
    e_neg = jnp.exp(-cum)
    e_rest = jnp.exp(tot - cum)
    at = -kk * jnp.exp(cum - lw)
    bt = b * e_neg
    kt = k_h * e_neg
    rt = r * jnp.exp(cum)
    bh = b * e_rest
    kh = k_h * e_rest
    ptot = jnp.exp(tot)

    lane = lax.broadcasted_iota(jnp.int32, (c_len, LANES), 1)
    first = lane < half
    rr = lax.broadcasted_iota(jnp.int32, (LANES, LANES), 0)
    cc = lax.broadcasted_iota(jnp.int32, (LANES, LANES), 1)
    strict = rr > cc
    incl = rr >= cc
    eye = (rr == cc).astype(F32)
    zeros_ll = jnp.zeros((LANES, LANES), F32)
    steps = int(math.log2(c_len))

    def stack(x):
        return _cat0(jnp.where(first, x, 0.0), jnp.where(first, 0.0, x))

    s_cur = [s_ref[q] for q in range(RW_PAIRS)]
    ys = []
    group = min(RW_GROUP, nchunks)
    for g0 in range(0, nchunks, group):
        units = [(c, q) for c in range(g0, g0 + group) for q in range(RW_PAIRS)]
        every = range(len(units))

        def stacks(x):
            return [stack(x[c * c_len:(c + 1) * c_len, q * LANES:(q + 1) * LANES]) for c, q in units]

        la, lr, sb, sk, sv, sbh, skh = (stacks(x) for x in (at, rt, bt, kt, v, bh, kh))
        prod = [_mm(_cat0(la[u], lr[u]), _cat0(sb[u], sk[u]), NT_DIMS) for u in every]
        a_ab = [jnp.where(strict, prod[u][:LANES, :LANES], 0.0) for u in every]
        a_ak = [jnp.where(strict, prod[u][:LANES, LANES:], 0.0) for u in every]
        a_rb = [jnp.where(incl, prod[u][LANES:, :LANES], 0.0) for u in every]
        a_rk = [jnp.where(incl, prod[u][LANES:, LANES:], 0.0) for u in every]
        av = [_mm(a_ak[u], sv[u]) for u in every]
        pw = [_mm(a_ab[u], a_ab[u]) for u in every]
        inv = [eye + a_ab[u] for u in every]
        for m in range(1, steps):
            if m == steps - 1:
                res = [_mm(pw[u], inv[u]) for u in every]
                inv = [inv[u] + res[u] for u in every]
            else:
                res = [_mm(pw[u], _cat1(pw[u], inv[u])) for u in every]
                pw = [res[u][:, :LANES] for u in every]
                inv = [inv[u] + res[u][:, LANES:] for u in every]
        wmat = [_mm(inv[u], _cat1(la[u], av[u])) for u in every]
        z = [_cat0(wmat[u], _cat1(zeros_ll, sv[u])) for u in every]
        qy = [_mm(_cat1(a_rb[u], a_rk[u]), z[u]) for u in every]
        gn = [_mm(z[u], _cat0(sbh[u], skh[u]), TN_DIMS) for u in every]
        qt = [lr[u] + qy[u][:, :LANES] for u in every]
        qt = [qt[u][:c_len] + qt[u][c_len:] for u in every]
        y0 = [qy[u][:c_len, LANES:] + qy[u][c_len:, LANES:] for u in every]
        for ci in range(group):
            c = g0 + ci
            us = [ci * RW_PAIRS + q for q in range(RW_PAIRS)]
            y_pairs = [y0[u] + _mm(qt[u], s_cur[q], NT_DIMS) for q, u in enumerate(us)]
            upd = [_mm(s_cur[q], gn[u][:LANES]) for q, u in enumerate(us)]
            s_cur = [s_cur[q] * ptot[c * c_len:c * c_len + 1, q * LANES:(q + 1) * LANES] + upd[q] + gn[u][LANES:]
                     for q, u in enumerate(us)]
            ys.append(_cat1(*y_pairs))
    for q in range(RW_PAIRS):
        s_ref[q] = s_cur[q]
    y = _cat0(*ys) if nchunks > 1 else ys[0]

    inv_n = 1.0 / RW_HEAD_DIM
    mean = _pair_sums(y, ones_pair) * inv_n
    d = y - mean
    var = _pair_sums(d * d, ones_pair) * inv_n
    y = d * lax.rsqrt(var + RW_LN_EPS) * lnw_ref[...] + lnb_ref[...]
    out = (y + bonus) * g
    o_ref[...] = out[:valid]

    @pl.when(i == pl.num_programs(1) - 1)
    def _():
        for q in range(RW_PAIRS):
            sout_ref[0, 2 * q] = s_cur[q][:half, :half]
            sout_ref[0, 2 * q + 1] = s_cur[q][half:, half:]


def _rwkv_mixer(p_rw, prev_rows, s0, batch, seq, lp):
    n = batch * seq
    if seq % RW_CHUNK == 0:
        tm = valid = min(256, seq)
    else:
        assert seq < RW_CHUNK and seq % 8 == 0
        tm, valid = RW_CHUNK, seq
    nt = seq // valid
    idx = np.arange(tm)
    tril = (((idx[:, None] // RW_CHUNK) == (idx[None, :] // RW_CHUNK)) & (idx[None, :] <= idx[:, None]))
    lane = np.arange(LANES) // RW_HEAD_DIM
    ones_pair = lane[:, None] == lane[None, :]
    vec = lambda width: pl.BlockSpec((1, width), lambda b, i: (0, 0))
    full = lambda r, c: pl.BlockSpec((r, c), lambda b, i: (0, 0))
    state = pl.BlockSpec((1, RW_HEADS, RW_HEAD_DIM, RW_HEAD_DIM), lambda b, i: (b, 0, 0, 0))
    return pl.pallas_call(
        functools.partial(_rwkv_body, tm=tm, valid=valid),
        grid=(batch, nt),
        in_specs=[pl.BlockSpec((valid, RW_PROJ), lambda b, i: (b * nt + i, 0)),
                  pl.BlockSpec((1, 1, RW_PROJ), lambda b, i: (b, 0, 0)),
                  state,
                  vec(RW_PROJ), vec(RW_WIDTH), full(RW_DECAY_LORA, RW_WIDTH),
                  vec(RW_WIDTH), full(RW_AAA_LORA, RW_WIDTH), full(RW_GATE_LORA, RW_WIDTH),
                  vec(RW_WIDTH), vec(RW_WIDTH), vec(RW_WIDTH), vec(RW_WIDTH), vec(RW_WIDTH),
                  full(tm, tm), full(LANES, LANES)],
        out_specs=[pl.BlockSpec((valid, RW_WIDTH), lambda b, i: (b * nt + i, 0)), state],
        out_shape=[jax.ShapeDtypeStruct((n, RW_WIDTH), F32),
                   jax.ShapeDtypeStruct((batch, RW_HEADS, RW_HEAD_DIM, RW_HEAD_DIM), F32)],
        scratch_shapes=[pltpu.VMEM((1, RW_PROJ), F32),
                        pltpu.VMEM((RW_PAIRS, LANES, LANES), F32)],
        compiler_params=_params("parallel", "arbitrary"),
        name="rwkv",
    )(p_rw, prev_rows.reshape(batch, 1, RW_PROJ), s0,
      lp["mu"].reshape(1, RW_PROJ), lp["w0"].reshape(1, RW_WIDTH), lp["w2"],
      lp["a0"].reshape(1, RW_WIDTH), lp["a2"], lp["g2"],
      lp["k_k"].reshape(1, RW_WIDTH), lp["k_a"].reshape(1, RW_WIDTH), lp["r_k"].reshape(1, RW_WIDTH),
      lp["lnx_w"].reshape(1, RW_WIDTH), lp["lnx_b"].reshape(1, RW_WIDTH),
      jnp.asarray(tril, BF16), jnp.asarray(ones_pair, BF16))


def _lam_of(lam_ref, lam_init):
    lq1, lk1, lq2, lk2 = lam_ref[0:1, :], lam_ref[1:2, :], lam_ref[2:3, :], lam_ref[3:4, :]
    return (jnp.exp(jnp.sum(lq1 * lk1, axis=-1, keepdims=True))
            - jnp.exp(jnp.sum(lq2 * lk2, axis=-1, keepdims=True)) + lam_init)


def _diff_finish(acc1, l1, acc2, l2, lam, subln_w, lam_init):
    o = acc1 / l1 - lam * (acc2 / l2)
    o = o * lax.rsqrt(jnp.mean(o * o, axis=-1, keepdims=True) + SUBLN_EPS) * subln_w
    return o * (1.0 - lam_init)


def _attn_prompt_body(qi_ref, kj_ref, slopes_ref, q_ref, k_ref, vt_ref, lam_ref, subln_ref, o_ref,
                      m_ref, l_ref, acc_ref, bias_ref, *, lam_init, tq, tk):
    h = pl.program_id(1)
    t = pl.program_id(2)
    i = qi_ref[t]
    j = kj_ref[t]
    neg_slope = -slopes_ref[h] * LOG2E

    @pl.when(t == 0)
    def _():
        kpos = lax.broadcasted_iota(jnp.int32, (tk, tq), 0)
        qpos = lax.broadcasted_iota(jnp.int32, (tk, tq), 1)
        bias_ref[...] = (qpos - kpos).astype(F32) * neg_slope

    @pl.when(j == 0)
    def _():
        m_ref[...] = jnp.full(m_ref.shape, NEG_BIG, F32)
        l_ref[...] = jnp.zeros(l_ref.shape, F32)
        acc_ref[...] = jnp.zeros(acc_ref.shape, F32)

    off = jnp.full((1, 1), i * tq - j * tk, jnp.int32).astype(F32) * neg_slope

    def update(on_diagonal):
        q = q_ref[...]
        k = k_ref[...]
        vt = vt_ref[...]
        units = [(c, q0) for c in range(2) for q0 in range(0, tq, ATTN_Q_BLOCK)]
        every = range(len(units))
        ss = []
        for c, q0 in units:
            sl = slice(c * DA_HEAD_DIM, (c + 1) * DA_HEAD_DIM)
            qs = slice(q0, q0 + ATTN_Q_BLOCK)
            s = lax.dot_general(k[:, sl], q[qs, sl], NT_DIMS, preferred_element_type=F32) + bias_ref[:, qs]
            if on_diagonal:
                kpos = lax.broadcasted_iota(jnp.int32, (tk, ATTN_Q_BLOCK), 0)
                qpos = lax.broadcasted_iota(jnp.int32, (tk, ATTN_Q_BLOCK), 1) + q0
                s = jnp.where(qpos >= kpos, s, NEG_BIG)
            ss.append(s)
        m_prev = [m_ref[c, :, q0:q0 + ATTN_Q_BLOCK] for c, q0 in units]
        m_new = [jnp.maximum(m_prev[u], jnp.max(ss[u], axis=0, keepdims=True) + off) for u in every]
        ps = [jnp.exp2(ss[u] - (m_new[u] - off)) for u in every]
        pv = [jnp.dot(vt, ps[u].astype(BF16), preferred_element_type=F32) for u in every]
        for u, (c, q0) in enumerate(units):
            qs = slice(q0, q0 + ATTN_Q_BLOCK)
            alpha = jnp.exp2(m_prev[u] - m_new[u])
            l_ref[c, :, qs] = alpha * l_ref[c, :, qs] + jnp.sum(ps[u], axis=0, keepdims=True)
            acc_ref[c, :, qs] = alpha * acc_ref[c, :, qs] + pv[u]
            m_ref[c, :, qs] = m_new[u]

    @pl.when(j < i)
    def _():
        update(False)

    @pl.when(j == i)
    def _():
        update(True)
        lam = _lam_of(lam_ref, lam_init)
        o = acc_ref[0] / l_ref[0] - lam * (acc_ref[1] / l_ref[1])
        o = o * lax.rsqrt(jnp.mean(o * o, axis=0, keepdims=True) + SUBLN_EPS)
        o_ref[...] = o.T * subln_ref[...] * (1.0 - lam_init)


def _alibi_slopes():
    return jnp.exp2(-ALIBI_MAX_BIAS / DA_HEADS * jnp.arange(1, DA_HEADS + 1, dtype=F32))


ATTN_Q_BLOCK = 256


def _attn_prompt(q, k, vt, lam_params, subln_w, lam_init, batch, seq):
    n = batch * seq
    tq = tk = min(512, seq)
    assert tq % ATTN_Q_BLOCK == 0
    nq = seq // tq
    pairs = [(i, j) for i in range(nq) for j in range(i + 1)]
    qi = jnp.asarray([pr[0] for pr in pairs], jnp.int32)
    kj = jnp.asarray([pr[1] for pr in pairs], jnp.int32)
    body = functools.partial(_attn_prompt_body, lam_init=lam_init, tq=tq, tk=tk)
    grid_spec = pltpu.PrefetchScalarGridSpec(
        num_scalar_prefetch=3,
        grid=(batch, DA_HEADS, len(pairs)),
        in_specs=[pl.BlockSpec((tq, DA_V_DIM), lambda b, h, t, qi, kj, sl: (b * nq + qi[t], h)),
                  pl.BlockSpec((tk, DA_V_DIM), lambda b, h, t, qi, kj, sl: (b * nq + kj[t], h)),
                  pl.BlockSpec((DA_V_DIM, tk), lambda b, h, t, qi, kj, sl: (h, b * nq + kj[t])),
                  pl.BlockSpec((4, DA_HEAD_DIM), lambda b, h, t, qi, kj, sl: (0, 0)),
                  pl.BlockSpec((1, DA_V_DIM), lambda b, h, t, qi, kj, sl: (0, 0))],
        out_specs=pl.BlockSpec((tq, DA_V_DIM), lambda b, h, t, qi, kj, sl: (b * nq + qi[t], h)),
        scratch_shapes=[pltpu.VMEM((2, 1, tq), F32), pltpu.VMEM((2, 1, tq), F32),
                        pltpu.VMEM((2, DA_V_DIM, tq), F32), pltpu.VMEM((tk, tq), F32)],
    )
    return pl.pallas_call(
        body,
        grid_spec=grid_spec,
        out_shape=jax.ShapeDtypeStruct((n, DA_WIDTH), F32),
        compiler_params=_params("parallel", "parallel", "arbitrary"),
        name="attn_prompt",
    )(qi, kj, _alibi_slopes(), q, k, vt, lam_params, subln_w.reshape(1, DA_V_DIM))


def _attn_sample_body(pt_ref, slopes_ref, q_ref, kn_ref, vn_ref, lam_ref, subln_ref, ck_hbm, cv_hbm, o_ref,
                      kbuf, vbuf, sem, qbd_ref, m_ref, l_ref, acc_ref,
                      *, lam_init, layer, group, page_size, n_pages, n_new):
    b = pl.program_id(0)
    rows2 = 2 * n_new
    n_past = n_pages * page_size
    n_groups = n_pages // group
    scale = DA_HEAD_DIM ** -0.5
    heads = range(DA_HEADS)
    pages = range(group)

    def group_copies(g, slot):
        copies = []
        for u in pages:
            page = pt_ref[b, g * group + u]
            copies.append(pltpu.make_async_copy(ck_hbm.at[layer, page], kbuf.at[slot, u], sem.at[0, slot]))
            copies.append(pltpu.make_async_copy(cv_hbm.at[layer, page], vbuf.at[slot, u], sem.at[1, slot]))
        return copies

    for cp in group_copies(0, 0):
        cp.start()

    q = q_ref[...] * scale
    lane = lax.broadcasted_iota(jnp.int32, (n_new, DA_V_DIM), 1)
    for h in heads:
        qh = q[:, h * DA_V_DIM:(h + 1) * DA_V_DIM]
        top = jnp.where(lane < DA_HEAD_DIM, qh, 0.0)
        bot = jnp.where(lane >= DA_HEAD_DIM, qh, 0.0)
        qbd_ref[h] = jnp.concatenate([top, bot], axis=0)
    m_ref[...] = jnp.full(m_ref.shape, NEG_BIG, F32)
    l_ref[...] = jnp.zeros(l_ref.shape, F32)
    acc_ref[...] = jnp.zeros(acc_ref.shape, F32)

    qpos = n_past + (lax.broadcasted_iota(jnp.int32, (rows2, page_size), 0) % n_new)
    key_lane = lax.broadcasted_iota(jnp.int32, (rows2, page_size), 1)
    head_rows = [pl.ds(h, page_size, stride=DA_HEADS) for h in heads]

    def group_step(g, carry):
        slot = lax.rem(g, 2)

        @pl.when(g + 1 < n_groups)
        def _():
            for cp in group_copies(g + 1, 1 - slot):
                cp.start()

        for cp in group_copies(g, slot):
            cp.wait()

        kpos0 = key_lane + g * (group * page_size)
        dists = [(qpos - (kpos0 + u * page_size)).astype(F32) for u in pages]
        qbd = [qbd_ref[h].astype(BF16) for h in heads]
        ss = [[lax.dot_general(qbd[h], kbuf[slot, u, head_rows[h], :].astype(BF16), NT_DIMS,
                               preferred_element_type=F32) - slopes_ref[h] * dists[u]
               for u in pages] for h in heads]
        m_prev = [m_ref[h] for h in heads]
        m_new = [jnp.maximum(m_prev[h], jnp.max(functools.reduce(jnp.maximum, ss[h]), axis=-1, keepdims=True))
                 for h in heads]
        ps = [[jnp.exp(ss[h][u] - m_new[h]) for u in pages] for h in heads]
        pv = [None] * DA_HEADS
        for u in pages:
            for h in heads:
                d = jnp.dot(ps[h][u].astype(BF16), vbuf[slot, u, head_rows[h], :].astype(BF16),
                            preferred_element_type=F32)
                pv[h] = d if pv[h] is None else pv[h] + d
        for h in heads:
            alpha = jnp.exp(m_prev[h] - m_new[h])
            l_ref[h] = alpha * l_ref[h] + jnp.sum(functools.reduce(jnp.add, ps[h]), axis=-1, keepdims=True)
            acc_ref[h] = alpha * acc_ref[h] + pv[h]
            m_ref[h] = m_new[h]
        return carry

    lax.fori_loop(0, n_groups, group_step, 0)

    lam = _lam_of(lam_ref, lam_init)
    qn = lax.broadcasted_iota(jnp.int32, (rows2, n_new), 0) % n_new
    kn = lax.broadcasted_iota(jnp.int32, (rows2, n_new), 1)
    dist_new = qn - kn
    for h in heads:
        hs = slice(h * DA_V_DIM, (h + 1) * DA_V_DIM)
        s = lax.dot_general(qbd_ref[h], kn_ref[:, hs], NT_DIMS, preferred_element_type=F32)
        s = s - slopes_ref[h] * dist_new.astype(F32)
        s = jnp.where(dist_new >= 0, s, NEG_BIG)
        m_prev = m_ref[h]
        m_new = jnp.maximum(m_prev, jnp.max(s, axis=-1, keepdims=True))
        alpha = jnp.exp(m_prev - m_new)
        p = jnp.exp(s - m_new)
        l = alpha * l_ref[h] + jnp.sum(p, axis=-1, keepdims=True)
        acc = alpha * acc_ref[h] + jnp.dot(p, vn_ref[:, hs], preferred_element_type=F32)
        o_ref[:, hs] = _diff_finish(acc[:n_new], l[:n_new], acc[n_new:], l[n_new:], lam,
                                    subln_ref[...], lam_init)


DECODE_PAGE_GROUP = 8


def _attn_sample(q, k_new, v_new, cache_k, cache_v, page_table, lam_params, subln_w, lam_init, layer,
                 dec_batch, dec_seq):
    page_rows = cache_k.shape[2]
    page_size = page_rows // DA_HEADS
    n_pages = page_table.shape[1]
    group = math.gcd(DECODE_PAGE_GROUP, n_pages)
    body = functools.partial(_attn_sample_body, lam_init=lam_init, layer=layer, group=group,
                             page_size=page_size, n_pages=n_pages, n_new=dec_seq)
    tok = pl.BlockSpec((dec_seq, DA_QK), lambda b, pt, sl: (b, 0))
    grid_spec = pltpu.PrefetchScalarGridSpec(
        num_scalar_prefetch=2,
        grid=(dec_batch,),
        in_specs=[tok, tok, tok,
                  pl.BlockSpec((4, DA_HEAD_DIM), lambda b, pt, sl: (0, 0)),
                  pl.BlockSpec((1, DA_V_DIM), lambda b, pt, sl: (0, 0)),
                  pl.BlockSpec(memory_space=pl.ANY),
                  pl.BlockSpec(memory_space=pl.ANY)],
        out_specs=tok,
        scratch_shapes=[pltpu.VMEM((2, group, page_rows, DA_V_DIM), F32),
                        pltpu.VMEM((2, group, page_rows, DA_V_DIM), F32),
                        pltpu.SemaphoreType.DMA((2, 2)),
                        pltpu.VMEM((DA_HEADS, 2 * dec_seq, DA_V_DIM), F32),
                        pltpu.VMEM((DA_HEADS, 2 * dec_seq, 1), F32),
                        pltpu.VMEM((DA_HEADS, 2 * dec_seq, 1), F32),
                        pltpu.VMEM((DA_HEADS, 2 * dec_seq, DA_V_DIM), F32)],
    )
    return pl.pallas_call(
        body,
        grid_spec=grid_spec,
        out_shape=jax.ShapeDtypeStruct((dec_batch * dec_seq, DA_WIDTH), F32),
        compiler_params=_params("arbitrary"),
        name="attn_sample",
    )(page_table, _alibi_slopes(), q, k_new, v_new, lam_params, subln_w.reshape(1, DA_V_DIM),
      cache_k, cache_v)


FFN_CHUNK = 256


def _ffn_body(x_ref, orw_ref, oda_ref, wo_ref, g2_ref, w1_ref, w3_ref, w2_ref, fg_ref, out_ref, *, final):
    x1 = (x_ref[...]
          + jnp.dot(orw_ref[...].astype(BF16), wo_ref[:RW_WIDTH, :], preferred_element_type=F32)
          + jnp.dot(oda_ref[...].astype(BF16), wo_ref[RW_WIDTH:, :], preferred_element_type=F32))
    hf = _rms(x1, g2_ref[...], NORM_EPS).astype(BF16)
    acc = jnp.zeros_like(x1)
    for c0 in range(0, D_FF, FFN_CHUNK):
        a = jnp.dot(hf, w1_ref[:, c0:c0 + FFN_CHUNK], preferred_element_type=F32)
        b = jnp.dot(hf, w3_ref[:, c0:c0 + FFN_CHUNK], preferred_element_type=F32)
        z = (a * jax.nn.sigmoid(a) * b).astype(BF16)
        acc = acc + jnp.dot(z, w2_ref[c0:c0 + FFN_CHUNK, :], preferred_element_type=F32)
    x2 = x1 + acc
    out_ref[...] = _rms(x2, fg_ref[...], NORM_EPS) if final else x2


def _outproj_ffn(x2d, o_rw, o_da, wo, g2, w1, w3, w2, final_g, final):
    n = x2d.shape[0]
    tm = min(512, n)
    row = lambda width: pl.BlockSpec((tm, width), lambda i: (i, 0))
    full = lambda r, c: pl.BlockSpec((r, c), lambda i: (0, 0))
    return pl.pallas_call(
        functools.partial(_ffn_body, final=final),
        grid=(n // tm,),
        in_specs=[row(D_MODEL), row(RW_WIDTH), row(DA_WIDTH), full(D_MODEL, D_MODEL), full(1, D_MODEL),
                  full(D_MODEL, D_FF), full(D_MODEL, D_FF), full(D_FF, D_MODEL), full(1, D_MODEL)],
        out_specs=row(D_MODEL),
        out_shape=jax.ShapeDtypeStruct((n, D_MODEL), F32),
        compiler_params=_params("parallel"),
        name="outproj_ffn",
    )(x2d, o_rw, o_da, wo, g2.reshape(1, D_MODEL), w1, w3, w2, final_g.reshape(1, D_MODEL))


def _lambda_init(layer):
    return 0.8 - 0.6 * math.exp(-0.3 * layer)


def kernel(x_prompt, x_sample, cache_k, cache_v, state_wkv, state_shift, page_table, norm1_g, w_in, rw_mu, rw_w0, rw_w2, rw_a0, rw_a2, rw_g2, rw_k_k, rw_k_a, rw_r_k, rw_lnx_w, rw_lnx_b, da_lam_q1, da_lam_k1, da_lam_q2, da_lam_k2, da_subln_w, w_out, norm2_g, ffn_w1, ffn_w3, ffn_w2, final_g):
    depth = w_in.shape[0]
    batch, seq, _ = x_prompt.shape
    dec_batch, dec_seq, _ = x_sample.shape
    n_pool, page_size = cache_k.shape[1], cache_k.shape[2]
    ck = cache_k.reshape(depth, n_pool, page_size * DA_HEADS, 2 * DA_HEAD_DIM)
    cv = cache_v.reshape(depth, n_pool, page_size * DA_HEADS, DA_V_DIM)

    xp = x_prompt.reshape(batch * seq, D_MODEL)
    xs = x_sample.reshape(dec_batch * dec_seq, D_MODEL)
    outs = {name: [] for name in ("sp", "shp", "ks", "vs", "ss", "shs")}
    kv_prompt = []
    for l in range(depth):
        lam_init = _lambda_init(l)
        lp = dict(mu=rw_mu[l], w0=rw_w0[l], w2=rw_w2[l], a0=rw_a0[l], a2=rw_a2[l], g2=rw_g2[l],
                  k_k=rw_k_k[l], k_a=rw_k_a[l], r_k=rw_r_k[l].reshape(RW_WIDTH),
                  lnx_w=rw_lnx_w[l], lnx_b=rw_lnx_b[l])
        lam_params = jnp.stack([da_lam_q1[l], da_lam_k1[l], da_lam_q2[l], da_lam_k2[l]])
        w_in_b = w_in[l].astype(BF16)
        wo_b, w1_b, w3_b, w2_b = (w_out[l].astype(BF16), ffn_w1[l].astype(BF16),
                                  ffn_w3[l].astype(BF16), ffn_w2[l].astype(BF16))
        final = l == depth - 1

        p_rw, k_new, v_new, q_b, k_b, vt_b = _inproj_prompt(xp, norm1_g[l], w_in_b, kv_prompt if final else [])
        kv_prompt.append((k_new, v_new))
        o_rw, s_new = _rwkv_mixer(p_rw, jnp.zeros((batch, RW_PROJ), F32),
                                  jnp.zeros((batch, RW_HEADS, RW_HEAD_DIM, RW_HEAD_DIM), F32),
                                  batch, seq, lp)
        o_da = _attn_prompt(q_b, k_b, vt_b, lam_params, da_subln_w[l], lam_init, batch, seq)
        xp = _outproj_ffn(xp, o_rw, o_da, wo_b, norm2_g[l], w1_b, w3_b, w2_b, final_g, final)
        outs["sp"].append(s_new)
        outs["shp"].append(p_rw.reshape(batch, seq, RW_PROJ)[:, -1])

        p_rw, q, k, v = _inproj(xs, norm1_g[l], w_in_b)
        o_rw, s_new = _rwkv_mixer(p_rw, state_shift[l], state_wkv[l], dec_batch, dec_seq, lp)
        o_da = _attn_sample(q, k, v, ck, cv, page_table, lam_params, da_subln_w[l], lam_init, l,
                            dec_batch, dec_seq)
        xs = _outproj_ffn(xs, o_rw, o_da, wo_b, norm2_g[l], w1_b, w3_b, w2_b, final_g, final)
        outs["ks"].append(k.reshape(dec_batch, dec_seq, DA_HEADS, 2 * DA_HEAD_DIM))
        outs["vs"].append(v.reshape(dec_batch, dec_seq, DA_HEADS, DA_V_DIM))
        outs["ss"].append(s_new)
        outs["shs"].append(p_rw.reshape(dec_batch, dec_seq, RW_PROJ)[:, -1])

    kp = kv_prompt[-1][0].reshape(depth, batch, seq, DA_HEADS, 2 * DA_HEAD_DIM)
    vp = kv_prompt[-1][1].reshape(depth, batch, seq, DA_HEADS, DA_V_DIM)
    return (xp.reshape(batch, seq, D_MODEL), xs.reshape(dec_batch, dec_seq, D_MODEL),
            kp, vp, jnp.stack(outs["sp"]), jnp.stack(outs["shp"]),
            jnp.stack(outs["ks"]), jnp.stack(outs["vs"]), jnp.stack(outs["ss"]), jnp.stack(outs["shs"]))
```

```python
import functools
import math

import jax
import jax.numpy as jnp
import numpy as np
from jax import lax
from jax.experimental import pallas as pl
from jax.experimental.pallas import tpu as pltpu

F32 = jnp.float32
BF16 = jnp.bfloat16

D_MODEL = 1024
RW_HEAD_DIM = 64
RW_WIDTH = 512
RW_HEADS = RW_WIDTH // RW_HEAD_DIM
RW_DECAY_LORA = 64
RW_AAA_LORA = 64
RW_GATE_LORA = 128
RW_LN_EPS = 64e-5
RW_PROJ = 3 * RW_WIDTH + RW_DECAY_LORA + RW_AAA_LORA + RW_GATE_LORA
DA_HEAD_DIM = 64
DA_V_DIM = 2 * DA_HEAD_DIM
DA_HEADS = 4
DA_QK = DA_HEADS * 2 * DA_HEAD_DIM
DA_WIDTH = DA_HEADS * DA_V_DIM
ALIBI_MAX_BIAS = 8.0
D_FF = 2816
NORM_EPS = 1e-5
SUBLN_EPS = 1e-5
NEG_BIG = -1e30
LOG2E = 1.4426950408889634

LANES = 128
RW_PAIRS = RW_WIDTH // LANES
RW_CHUNK = 64
RW_GROUP = 2

VMEM_LIMIT_BYTES = 56 * 1024 * 1024

NT_DIMS = (((1,), (1,)), ((), ()))
TN_DIMS = (((0,), (0,)), ((), ()))


def _params(*semantics):
    return pltpu.CompilerParams(dimension_semantics=semantics,
                                vmem_limit_bytes=VMEM_LIMIT_BYTES)


def _rms(x, g, eps):
    return x * lax.rsqrt(jnp.mean(x * x, axis=-1, keepdims=True) + eps) * g


def _mm(a, b, dims=None):
    a, b = a.astype(BF16), b.astype(BF16)
    if dims is None:
        return jnp.dot(a, b, preferred_element_type=F32)
    return lax.dot_general(a, b, dims, preferred_element_type=F32)


def _split_terms(x, terms):
    out = []
    for _ in range(terms):
        hi = x.astype(BF16)
        out.append(hi)
        x = x - hi.astype(F32)
    return out


def _inproj_body(x_ref, g_ref, w_ref, prw_ref, q_ref, k_ref, v_ref):
    h = _rms(x_ref[...], g_ref[...], NORM_EPS).astype(BF16)
    c0, c1, c2 = RW_PROJ, RW_PROJ + DA_QK, RW_PROJ + 2 * DA_QK
    prw_ref[...] = jnp.dot(h, w_ref[:, :c0], preferred_element_type=F32)
    q_ref[...] = jnp.dot(h, w_ref[:, c0:c1], preferred_element_type=F32)
    k_ref[...] = jnp.dot(h, w_ref[:, c1:c2], preferred_element_type=F32)
    v_ref[...] = jnp.dot(h, w_ref[:, c2:], preferred_element_type=F32)


def _inproj_prompt_body(x_ref, g_ref, w_ref, wvt_ref, *rest, n_prev):
    prev = rest[:2 * n_prev]
    prw_ref, k_ref, v_ref, qb_ref, kb_ref, vtb_ref = rest[2 * n_prev:]
    if n_prev:
        for li in range(n_prev):
            k_ref[li] = prev[2 * li][...]
            v_ref[li] = prev[2 * li + 1][...]
        k_ref, v_ref = k_ref.at[n_prev], v_ref.at[n_prev]
    tm = x_ref.shape[0]
    h = _rms(x_ref[...], g_ref[...], NORM_EPS).astype(BF16)
    c0, c1, c2 = RW_PROJ, RW_PROJ + DA_QK, RW_PROJ + 2 * DA_QK
    prw_ref[...] = jnp.dot(h, w_ref[:, :c0], preferred_element_type=F32)
    q = jnp.dot(h, w_ref[:, c0:c1], preferred_element_type=F32)
    qb_ref[...] = (q * (DA_HEAD_DIM ** -0.5 * LOG2E)).astype(BF16)
    k = jnp.dot(h, w_ref[:, c1:c2], preferred_element_type=F32)
    kb_ref[...] = k.astype(BF16)
    v = jnp.dot(h, w_ref[:, c2:], preferred_element_type=F32)
    for hd in range(DA_HEADS):
        head_rows = pl.ds(hd, tm, stride=DA_HEADS)
        k_ref[head_rows, :] = k[:, hd * DA_V_DIM:(hd + 1) * DA_V_DIM]
        v_ref[head_rows, :] = v[:, hd * DA_V_DIM:(hd + 1) * DA_V_DIM]
    vtb_ref[...] = lax.dot_general(wvt_ref[...], h, NT_DIMS, preferred_element_type=F32).astype(BF16)


def _inproj_prompt(x2d, g, w_bf16, kv_prev):
    n = x2d.shape[0]
    tm = min(512, n)
    in_proj = w_bf16.shape[1]
    wvt = w_bf16[:, RW_PROJ + 2 * DA_QK:].T
    n_prev = len(kv_prev)
    row = lambda width: pl.BlockSpec((tm, width), lambda i: (i, 0))
    cache_rows = pl.BlockSpec((tm * DA_HEADS, DA_V_DIM), lambda i: (i, 0))
    if n_prev:
        cache_out = pl.BlockSpec((n_prev + 1, tm * DA_HEADS, DA_V_DIM), lambda i: (0, i, 0))
        cache_shape = jax.ShapeDtypeStruct((n_prev + 1, n * DA_HEADS, DA_V_DIM), F32)
    else:
        cache_out = cache_rows
        cache_shape = jax.ShapeDtypeStruct((n * DA_HEADS, DA_V_DIM), F32)
    return pl.pallas_call(
        functools.partial(_inproj_prompt_body, n_prev=n_prev),
        grid=(n // tm,),
        in_specs=[row(D_MODEL),
                  pl.BlockSpec((1, D_MODEL), lambda i: (0, 0)),
                  pl.BlockSpec((D_MODEL, in_proj), lambda i: (0, 0)),
                  pl.BlockSpec((DA_WIDTH, D_MODEL), lambda i: (0, 0))] + [cache_rows] * (2 * n_prev),
        out_specs=[row(RW_PROJ), cache_out, cache_out, row(DA_QK), row(DA_QK),
                   pl.BlockSpec((DA_WIDTH, tm), lambda i: (0, i))],
        out_shape=[jax.ShapeDtypeStruct((n, RW_PROJ), F32), cache_shape, cache_shape,
                   jax.ShapeDtypeStruct((n, DA_QK), BF16),
                   jax.ShapeDtypeStruct((n, DA_QK), BF16),
                   jax.ShapeDtypeStruct((DA_WIDTH, n), BF16)],
        compiler_params=_params("parallel"),
        name="inproj_prompt",
    )(x2d, g.reshape(1, D_MODEL), w_bf16, wvt, *[a for kv in kv_prev for a in kv])


def _inproj(x2d, g, w_bf16):
    n = x2d.shape[0]
    tm = min(512, n)
    in_proj = w_bf16.shape[1]
    row = lambda width: pl.BlockSpec((tm, width), lambda i: (i, 0))
    return pl.pallas_call(
        _inproj_body,
        grid=(n // tm,),
        in_specs=[row(D_MODEL),
                  pl.BlockSpec((1, D_MODEL), lambda i: (0, 0)),
                  pl.BlockSpec((D_MODEL, in_proj), lambda i: (0, 0))],
        out_specs=[row(RW_PROJ), row(DA_QK), row(DA_QK), row(DA_WIDTH)],
        out_shape=[jax.ShapeDtypeStruct((n, RW_PROJ), F32),
                   jax.ShapeDtypeStruct((n, DA_QK), F32),
                   jax.ShapeDtypeStruct((n, DA_QK), F32),
                   jax.ShapeDtypeStruct((n, DA_WIDTH), F32)],
        compiler_params=_params("parallel"),
        name="inproj",
    )(x2d, g.reshape(1, D_MODEL), w_bf16)


def _softplus(z):
    return jnp.maximum(z, 0.0) + jnp.log(1.0 + jnp.exp(-jnp.abs(z)))


def _pair_sums(x, ones_pair):
    cols = []
    for q in range(RW_PAIRS):
        terms = _split_terms(x[:, q * LANES:(q + 1) * LANES], 2)
        cols.append(sum(jnp.dot(t, ones_pair, preferred_element_type=F32) for t in terms))
    return jnp.concatenate(cols, axis=-1)


def _cat0(*xs):
    return jnp.concatenate(xs, axis=0)


def _cat1(*xs):
    return jnp.concatenate(xs, axis=1)


def _rwkv_body(p_ref, prev_ref, s0_ref, mu_ref, w0_ref, w2_ref, a0_ref, a2_ref, g2_ref,
               kk_ref, ka_ref, rk_ref, lnw_ref, lnb_ref, tril_ref, ones_ref,
               o_ref, sout_ref, carry_ref, s_ref, *, tm, valid):
    i = pl.program_id(1)
    c_len = RW_CHUNK
    half = RW_HEAD_DIM
    zeros_hh = jnp.zeros((half, half), F32)

    @pl.when(i == 0)
    def _():
        carry_ref[...] = prev_ref[0]
        for q in range(RW_PAIRS):
            top = _cat1(s0_ref[0, 2 * q], zeros_hh)
            bot = _cat1(zeros_hh, s0_ref[0, 2 * q + 1])
            s_ref[q] = _cat0(top, bot)

    p = p_ref[...]
    prev_row = carry_ref[...]
    carry_ref[...] = p[valid - 1:valid, :]
    if valid < tm:
        p = _cat0(p, jnp.zeros((tm - valid, RW_PROJ), F32))
    rolled = pltpu.roll(p, 1, axis=0)
    row = lax.broadcasted_iota(jnp.int32, (tm, 1), 0)
    prev = jnp.where(row == 0, prev_row, rolled)
    xs = p + (prev - p) * mu_ref[...]

    c1, c2, c3 = RW_WIDTH, 2 * RW_WIDTH, 3 * RW_WIDTH
    c4 = c3 + RW_DECAY_LORA
    c5 = c4 + RW_AAA_LORA
    r, k, v = xs[:, :c1], xs[:, c1:c2], xs[:, c2:c3]
    wd, ad, gd = xs[:, c3:c4], xs[:, c4:c5], xs[:, c5:]

    w = -_softplus(-(w0_ref[...] + jnp.dot(jnp.tanh(wd), w2_ref[...], preferred_element_type=F32))) - 0.5
    lw = -jnp.exp(w)
    a = jax.nn.sigmoid(a0_ref[...] + jnp.dot(ad, a2_ref[...], preferred_element_type=F32))
    g = jnp.dot(jax.nn.sigmoid(gd), g2_ref[...], preferred_element_type=F32)

    ones_pair = ones_ref[...]
    kk = k * kk_ref[...]
    kk = kk / jnp.maximum(jnp.sqrt(_pair_sums(kk * kk, ones_pair)), 1e-12)
    k_h = k * (1.0 + (a - 1.0) * ka_ref[...])
    bonus = _pair_sums(r * k_h * rk_ref[...], ones_pair) * v
    b = kk * a

    if valid < tm:
        live = row < valid
        lw = jnp.where(live, lw, 0.0)
        kk, b, k_h, r, v = (jnp.where(live, t, 0.0) for t in (kk, b, k_h, r, v))

    tril = tril_ref[...]
    cum = sum(jnp.dot(tril, t, preferred_element_type=F32) for t in _split_terms(lw, 3))
    nchunks = tm // c_len
    tot = _cat0(*[jnp.broadcast_to(cum[(c + 1) * c_len - 1:(c + 1) * c_len, :], (c_len, RW_WIDTH))
                  for c in range(nchunks)])
    e_neg = jnp.exp(-cum)
    e_rest = jnp.exp(tot - cum)
    at = -kk * jnp.exp(cum - lw)
    bt = b * e_neg
    kt = k_h * e_neg
    rt = r * jnp.exp(cum)
    bh = b * e_rest
    kh = k_h * e_rest
    ptot = jnp.exp(tot)

    lane = lax.broadcasted_iota(jnp.int32, (c_len, LANES), 1)
    first = lane < half
    rr = lax.broadcasted_iota(jnp.int32, (LANES, LANES), 0)
    cc = lax.broadcasted_iota(jnp.int32, (LANES, LANES), 1)
    strict = rr > cc
    incl = rr >= cc
    eye = (rr == cc).astype(F32)
    zeros_ll = jnp.zeros((LANES, LANES), F32)
    steps = int(math.log2(c_len))

    def stack(x):
        return _cat0(jnp.where(first, x, 0.0), jnp.where(first, 0.0, x))

    s_cur = [s_ref[q] for q in range(RW_PAIRS)]
    ys = []
    group = min(RW_GROUP, nchunks)
    for g0 in range(0, nchunks, group):
        units = [(c, q) for c in range(g0, g0 + group) for q in range(RW_PAIRS)]
        every = range(len(units))

        def stacks(x):
            return [stack(x[c * c_len:(c + 1) * c_len, q * LANES:(q + 1) * LANES]) for c, q in units]

        la, lr, sb, sk, sv, sbh, skh = (stacks(x) for x in (at, rt, bt, kt, v, bh, kh))
        prod = [_mm(_cat0(la[u], lr[u]), _cat0(sb[u], sk[u]), NT_DIMS) for u in every]
        a_ab = [jnp.where(strict, prod[u][:LANES, :LANES], 0.0) for u in every]
        a_ak = [jnp.where(strict, prod[u][:LANES, LANES:], 0.0) for u in every]
        a_rb = [jnp.where(incl, prod[u][LANES:, :LANES], 0.0) for u in every]
        a_rk = [jnp.where(incl, prod[u][LANES:, LANES:], 0.0) for u in every]
        av = [_mm(a_ak[u], sv[u]) for u in every]
        pw = [_mm(a_ab[u], a_ab[u]) for u in every]
        inv = [eye + a_ab[u] for u in every]
        for m in range(1, steps):
            if m == steps - 1:
                res = [_mm(pw[u], inv[u]) for u in every]
                inv = [inv[u] + res[u] for u in every]
            else:
                res = [_mm(pw[u], _cat1(pw[u], inv[u])) for u in every]
                pw = [res[u][:, :LANES] for u in every]
                inv = [inv[u] + res[u][:, LANES:] for u in every]
        wmat = [_mm(inv[u], _cat1(la[u], av[u])) for u in every]
        z = [_cat0(wmat[u], _cat1(zeros_ll, sv[u])) for u in every]
        qy = [_mm(_cat1(a_rb[u], a_rk[u]), z[u]) for u in every]
        gn = [_mm(z[u], _cat0(sbh[u], skh[u]), TN_DIMS) for u in every]
        qt = [lr[u] + qy[u][:, :LANES] for u in every]
        qt = [qt[u][:c_len] + qt[u][c_len:] for u in every]
        y0 = [qy[u][:c_len, LANES:] + qy[u][c_len:, LANES:] for u in every]
        for ci in range(group):
            c = g0 + ci
            us = [ci * RW_PAIRS + q for q in range(RW_PAIRS)]
            y_pairs = [y0[u] + _mm(qt[u], s_cur[q], NT_DIMS) for q, u in enumerate(us)]
            upd = [_mm(s_cur[q], gn[u][:LANES]) for q, u in enumerate(us)]
            s_cur = [s_cur[q] * ptot[c * c_len:c * c_len + 1, q * LANES:(q + 1) * LANES] + upd[q] + gn[u][LANES:]
                     for q, u in enumerate(us)]
            ys.append(_cat1(*y_pairs))
    for q in range(RW_PAIRS):
        s_ref[q] = s_cur[q]
    y = _cat0(*ys) if nchunks > 1 else ys[0]

    inv_n = 1.0 / RW_HEAD_DIM
    mean = _pair_sums(y, ones_pair) * inv_n
    d = y - mean
    var = _pair_sums(d * d, ones_pair) * inv_n
    y = d * lax.rsqrt(var + RW_LN_EPS) * lnw_ref[...] + lnb_ref[...]
    out = (y + bonus) * g
    o_ref[...] = out[:valid]

    @pl.when(i == pl.num_programs(1) - 1)
    def _():
        for q in range(RW_PAIRS):
            sout_ref[0, 2 * q] = s_cur[q][:half, :half]
            sout_ref[0, 2 * q + 1] = s_cur[q][half:, half:]


def _rwkv_mixer(p_rw, prev_rows, s0, batch, seq, lp):
    n = batch * seq
    if seq % RW_CHUNK == 0:
        tm = valid = min(256, seq)
    else:
        assert seq < RW_CHUNK and seq % 8 == 0
        tm, valid = RW_CHUNK, seq
    nt = seq // valid
    idx = np.arange(tm)
    tril = (((idx[:, None] // RW_CHUNK) == (idx[None, :] // RW_CHUNK)) & (idx[None, :] <= idx[:, None]))
    lane = np.arange(LANES) // RW_HEAD_DIM
    ones_pair = lane[:, None] == lane[None, :]
    vec = lambda width: pl.BlockSpec((1, width), lambda b, i: (0, 0))
    full = lambda r, c: pl.BlockSpec((r, c), lambda b, i: (0, 0))
    state = pl.BlockSpec((1, RW_HEADS, RW_HEAD_DIM, RW_HEAD_DIM), lambda b, i: (b, 0, 0, 0))
    return pl.pallas_call(
        functools.partial(_rwkv_body, tm=tm, valid=valid),
        grid=(batch, nt),
        in_specs=[pl.BlockSpec((valid, RW_PROJ), lambda b, i: (b * nt + i, 0)),
                  pl.BlockSpec((1, 1, RW_PROJ), lambda b, i: (b, 0, 0)),
                  state,
                  vec(RW_PROJ), vec(RW_WIDTH), full(RW_DECAY_LORA, RW_WIDTH),
                  vec(RW_WIDTH), full(RW_AAA_LORA, RW_WIDTH), full(RW_GATE_LORA, RW_WIDTH),
                  vec(RW_WIDTH), vec(RW_WIDTH), vec(RW_WIDTH), vec(RW_WIDTH), vec(RW_WIDTH),
                  full(tm, tm), full(LANES, LANES)],
        out_specs=[pl.BlockSpec((valid, RW_WIDTH), lambda b, i: (b * nt + i, 0)), state],
        out_shape=[jax.ShapeDtypeStruct((n, RW_WIDTH), F32),
                   jax.ShapeDtypeStruct((batch, RW_HEADS, RW_HEAD_DIM, RW_HEAD_DIM), F32)],
        scratch_shapes=[pltpu.VMEM((1, RW_PROJ), F32),
                        pltpu.VMEM((RW_PAIRS, LANES, LANES), F32)],
        compiler_params=_params("parallel", "arbitrary"),
        name="rwkv",
    )(p_rw, prev_rows.reshape(batch, 1, RW_PROJ), s0,
      lp["mu"].reshape(1, RW_PROJ), lp["w0"].reshape(1, RW_WIDTH), lp["w2"],
      lp["a0"].reshape(1, RW_WIDTH), lp["a2"], lp["g2"],
      lp["k_k"].reshape(1, RW_WIDTH), lp["k_a"].reshape(1, RW_WIDTH), lp["r_k"].reshape(1, RW_WIDTH),
      lp["lnx_w"].reshape(1, RW_WIDTH), lp["lnx_b"].reshape(1, RW_WIDTH),
      jnp.asarray(tril, BF16), jnp.asarray(ones_pair, BF16))


def _lam_of(lam_ref, lam_init):
    lq1, lk1, lq2, lk2 = lam_ref[0:1, :], lam_ref[1:2, :], lam_ref[2:3, :], lam_ref[3:4, :]
    return (jnp.exp(jnp.sum(lq1 * lk1, axis=-1, keepdims=True))
            - jnp.exp(jnp.sum(lq2 * lk2, axis=-1, keepdims=True)) + lam_init)


def _diff_finish(acc1, l1, acc2, l2, lam, subln_w, lam_init):
    o = acc1 / l1 - lam * (acc2 / l2)
    o = o * lax.rsqrt(jnp.mean(o * o, axis=-1, keepdims=True) + SUBLN_EPS) * subln_w
    return o * (1.0 - lam_init)


def _attn_prompt_body(qi_ref, kj_ref, slopes_ref, q_ref, k_ref, vt_ref, lam_ref, subln_ref, o_ref,
                      m_ref, l_ref, acc_ref, bias_ref, *, lam_init, tq, tk):
    hg = pl.program_id(1)
    t = pl.program_id(2)
    i = qi_ref[t]
    j = kj_ref[t]
    group_heads = range(ATTN_HEADS_PER_STEP)
    neg_slope = [-slopes_ref[hg * ATTN_HEADS_PER_STEP + hh] * LOG2E for hh in group_heads]

    @pl.when(t == 0)
    def _():
        kpos = lax.broadcasted_iota(jnp.int32, (tk, tq), 0)
        qpos = lax.broadcasted_iota(jnp.int32, (tk, tq), 1)
        dist = (qpos - kpos).astype(F32)
        for hh in group_heads:
            bias_ref[hh] = dist * neg_slope[hh]

    @pl.when(j == 0)
    def _():
        m_ref[...] = jnp.full(m_ref.shape, NEG_BIG, F32)
        l_ref[...] = jnp.zeros(l_ref.shape, F32)
        acc_ref[...] = jnp.zeros(acc_ref.shape, F32)

    tile_dist = jnp.full((1, 1), i * tq - j * tk, jnp.int32).astype(F32)
    off = [tile_dist * neg_slope[hh] for hh in group_heads]

    def update(on_diagonal):
        q = q_ref[...]
        k = k_ref[...]
        vt = vt_ref[...]
        units = [(hh, c, q0) for hh in group_heads for c in range(2) for q0 in range(0, tq, ATTN_Q_BLOCK)]
        every = range(len(units))
        ss = []
        for hh, c, q0 in units:
            sl = slice(hh * DA_V_DIM + c * DA_HEAD_DIM, hh * DA_V_DIM + (c + 1) * DA_HEAD_DIM)
            qs = slice(q0, q0 + ATTN_Q_BLOCK)
            s = lax.dot_general(k[:, sl], q[qs, sl], NT_DIMS, preferred_element_type=F32) + bias_ref[hh, :, qs]
            if on_diagonal:
                kpos = lax.broadcasted_iota(jnp.int32, (tk, ATTN_Q_BLOCK), 0)
                qpos = lax.broadcasted_iota(jnp.int32, (tk, ATTN_Q_BLOCK), 1) + q0
                s = jnp.where(qpos >= kpos, s, NEG_BIG)
            ss.append(s)
        m_prev = [m_ref[2 * hh + c, :, q0:q0 + ATTN_Q_BLOCK] for hh, c, q0 in units]
        m_new = [jnp.maximum(m_prev[u], jnp.max(ss[u], axis=0, keepdims=True) + off[units[u][0]]) for u in every]
        ps = [jnp.exp2(ss[u] - (m_new[u] - off[units[u][0]])) for u in every]
        pv = [jnp.dot(vt[units[u][0] * DA_V_DIM:(units[u][0] + 1) * DA_V_DIM], ps[u].astype(BF16),
                      preferred_element_type=F32) for u in every]
        for u, (hh, c, q0) in enumerate(units):
            qs = slice(q0, q0 + ATTN_Q_BLOCK)
            alpha = jnp.exp2(m_prev[u] - m_new[u])
            l_ref[2 * hh + c, :, qs] = alpha * l_ref[2 * hh + c, :, qs] + jnp.sum(ps[u], axis=0, keepdims=True)
            acc_ref[2 * hh + c, :, qs] = alpha * acc_ref[2 * hh + c, :, qs] + pv[u]
            m_ref[2 * hh + c, :, qs] = m_new[u]

    @pl.when(j < i)
    def _():
        update(False)

    @pl.when(j == i)
    def _():
        update(True)
        lam = _lam_of(lam_ref, lam_init)
        for hh in group_heads:
            o = acc_ref[2 * hh] / l_ref[2 * hh] - lam * (acc_ref[2 * hh + 1] / l_ref[2 * hh + 1])
            o = o * lax.rsqrt(jnp.mean(o * o, axis=0, keepdims=True) + SUBLN_EPS)
            o_ref[:, hh * DA_V_DIM:(hh + 1) * DA_V_DIM] = o.T * subln_ref[...] * (1.0 - lam_init)


def _alibi_slopes():
    return jnp.exp2(-ALIBI_MAX_BIAS / DA_HEADS * jnp.arange(1, DA_HEADS + 1, dtype=F32))


ATTN_Q_BLOCK = 256
ATTN_HEADS_PER_STEP = 2


def _attn_prompt(q, k, vt, lam_params, subln_w, lam_init, batch, seq):
    n = batch * seq
    tq = tk = min(512, seq)
    assert tq % ATTN_Q_BLOCK == 0 and DA_HEADS % ATTN_HEADS_PER_STEP == 0
    nq = seq // tq
    hps = ATTN_HEADS_PER_STEP
    width = hps * DA_V_DIM
    pairs = [(i, j) for i in range(nq) for j in range(i + 1)]
    qi = jnp.asarray([pr[0] for pr in pairs], jnp.int32)
    kj = jnp.asarray([pr[1] for pr in pairs], jnp.int32)
    body = functools.partial(_attn_prompt_body, lam_init=lam_init, tq=tq, tk=tk)
    grid_spec = pltpu.PrefetchScalarGridSpec(
        num_scalar_prefetch=3,
        grid=(batch, DA_HEADS // hps, len(pairs)),
        in_specs=[pl.BlockSpec((tq, width), lambda b, h, t, qi, kj, sl: (b * nq + qi[t], h)),
                  pl.BlockSpec((tk, width), lambda b, h, t, qi, kj, sl: (b * nq + kj[t], h)),
                  pl.BlockSpec((width, tk), lambda b, h, t, qi, kj, sl: (h, b * nq + kj[t])),
                  pl.BlockSpec((4, DA_HEAD_DIM), lambda b, h, t, qi, kj, sl: (0, 0)),
                  pl.BlockSpec((1, DA_V_DIM), lambda b, h, t, qi, kj, sl: (0, 0))],
        out_specs=pl.BlockSpec((tq, width), lambda b, h, t, qi, kj, sl: (b * nq + qi[t], h)),
        scratch_shapes=[pltpu.VMEM((2 * hps, 1, tq), F32), pltpu.VMEM((2 * hps, 1, tq), F32),
                        pltpu.VMEM((2 * hps, DA_V_DIM, tq), F32), pltpu.VMEM((hps, tk, tq), F32)],
    )
    return pl.pallas_call(
        body,
        grid_spec=grid_spec,
        out_shape=jax.ShapeDtypeStruct((n, DA_WIDTH), F32),
        compiler_params=_params("parallel", "parallel", "arbitrary"),
        name="attn_prompt",
    )(qi, kj, _alibi_slopes(), q, k, vt, lam_params, subln_w.reshape(1, DA_V_DIM))


def _attn_sample_body(pt_ref, slopes_ref, q_ref, kn_ref, vn_ref, lam_ref, subln_ref, ck_hbm, cv_hbm, o_ref,
                      kbuf, vbuf, sem, qbd_ref, m_ref, l_ref, acc_ref, slope_ref, bias_ref,
                      *, lam_init, layer, group, page_size, n_pages, n_new):
    b = pl.program_id(0)
    rows2 = 2 * n_new
    n_past = n_pages * page_size
    n_groups = n_pages // group
    scale = DA_HEAD_DIM ** -0.5
    heads = range(DA_HEADS)
    pages = range(group)

    def group_copies(g, slot):
        copies = []
        for u in pages:
            page = pt_ref[b, g * group + u]
            copies.append(pltpu.make_async_copy(ck_hbm.at[layer, page], kbuf.at[slot, u], sem.at[0, slot]))
            copies.append(pltpu.make_async_copy(cv_hbm.at[layer, page], vbuf.at[slot, u], sem.at[1, slot]))
        return copies

    for cp in group_copies(0, 0):
        cp.start()

    q = q_ref[...] * scale
    lane = lax.broadcasted_iota(jnp.int32, (n_new, DA_V_DIM), 1)
    for h in heads:
        qh = q[:, h * DA_V_DIM:(h + 1) * DA_V_DIM]
        top = jnp.where(lane < DA_HEAD_DIM, qh, 0.0)
        bot = jnp.where(lane >= DA_HEAD_DIM, qh, 0.0)
        qbd_ref[h] = jnp.concatenate([top, bot], axis=0)
    m_ref[...] = jnp.full(m_ref.shape, NEG_BIG, F32)
    l_ref[...] = jnp.zeros(l_ref.shape, F32)
    acc_ref[...] = jnp.zeros(acc_ref.shape, F32)

    all_rows = DA_HEADS * rows2
    page_rows = DA_HEADS * page_size

    @pl.when(b == 0)
    def _():
        r = lax.broadcasted_iota(jnp.int32, (all_rows, page_rows), 0)
        c = lax.broadcasted_iota(jnp.int32, (all_rows, page_rows), 1)
        row_head = r // rows2
        slope = jnp.zeros((all_rows, page_rows), F32)
        for h in heads:
            slope = jnp.where(row_head == h, slopes_ref[h], slope)
        dist = (n_past + r % n_new - c // DA_HEADS).astype(F32)
        slope_ref[...] = slope
        bias_ref[...] = jnp.where(c % DA_HEADS == row_head, -slope * dist, NEG_BIG)

    def group_step(g, carry):
        slot = lax.rem(g, 2)

        @pl.when(g + 1 < n_groups)
        def _():
            for cp in group_copies(g + 1, 1 - slot):
                cp.start()

        for cp in group_copies(g, slot):
            cp.wait()

        q_all = qbd_ref[...].reshape(all_rows, DA_V_DIM).astype(BF16)
        slope = slope_ref[...]
        ss = []
        for u in pages:
            first_key = jnp.full((1, 1), (g * group + u) * page_size, jnp.int32).astype(F32)
            s = lax.dot_general(q_all, kbuf[slot, u].astype(BF16), NT_DIMS, preferred_element_type=F32)
            ss.append(s + (bias_ref[...] + slope * first_key))
        m_prev = m_ref[...].reshape(all_rows, 1)
        m_new = jnp.maximum(m_prev, jnp.max(functools.reduce(jnp.maximum, ss), axis=-1, keepdims=True))
        ps = [jnp.exp(s - m_new) for s in ss]
        pv = None
        for u in pages:
            d = jnp.dot(ps[u].astype(BF16), vbuf[slot, u].astype(BF16), preferred_element_type=F32)
            pv = d if pv is None else pv + d
        alpha = jnp.exp(m_prev - m_new)
        l_new = alpha * l_ref[...].reshape(all_rows, 1) + jnp.sum(functools.reduce(jnp.add, ps), axis=-1,
                                                                   keepdims=True)
        acc_new = alpha * acc_ref[...].reshape(all_rows, DA_V_DIM) + pv
        l_ref[...] = l_new.reshape(DA_HEADS, rows2, 1)
        acc_ref[...] = acc_new.reshape(DA_HEADS, rows2, DA_V_DIM)
        m_ref[...] = m_new.reshape(DA_HEADS, rows2, 1)
        return carry

    lax.fori_loop(0, n_groups, group_step, 0)

    lam = _lam_of(lam_ref, lam_init)
    qn = lax.broadcasted_iota(jnp.int32, (rows2, n_new), 0) % n_new
    kn = lax.broadcasted_iota(jnp.int32, (rows2, n_new), 1)
    dist_new = qn - kn
    for h in heads:
        hs = slice(h * DA_V_DIM, (h + 1) * DA_V_DIM)
        s = lax.dot_general(qbd_ref[h], kn_ref[:, hs], NT_DIMS, preferred_element_type=F32)
        s = s - slopes_ref[h] * dist_new.astype(F32)
        s = jnp.where(dist_new >= 0, s, NEG_BIG)
        m_prev = m_ref[h]
        m_new = jnp.maximum(m_prev, jnp.max(s, axis=-1, keepdims=True))
        alpha = jnp.exp(m_prev - m_new)
        p = jnp.exp(s - m_new)
        l = alpha * l_ref[h] + jnp.sum(p, axis=-1, keepdims=True)
        acc = alpha * acc_ref[h] + jnp.dot(p, vn_ref[:, hs], preferred_element_type=F32)
        o_ref[:, hs] = _diff_finish(acc[:n_new], l[:n_new], acc[n_new:], l[n_new:], lam,
                                    subln_ref[...], lam_init)


DECODE_PAGE_GROUP = 8


def _attn_sample(q, k_new, v_new, cache_k, cache_v, page_table, lam_params, subln_w, lam_init, layer,
                 dec_batch, dec_seq):
    page_rows = cache_k.shape[2]
    page_size = page_rows // DA_HEADS
    n_pages = page_table.shape[1]
    group = math.gcd(DECODE_PAGE_GROUP, n_pages)
    body = functools.partial(_attn_sample_body, lam_init=lam_init, layer=layer, group=group,
                             page_size=page_size, n_pages=n_pages, n_new=dec_seq)
    tok = pl.BlockSpec((dec_seq, DA_QK), lambda b, pt, sl: (b, 0))
    grid_spec = pltpu.PrefetchScalarGridSpec(
        num_scalar_prefetch=2,
        grid=(dec_batch,),
        in_specs=[tok, tok, tok,
                  pl.BlockSpec((4, DA_HEAD_DIM), lambda b, pt, sl: (0, 0)),
                  pl.BlockSpec((1, DA_V_DIM), lambda b, pt, sl: (0, 0)),
                  pl.BlockSpec(memory_space=pl.ANY),
                  pl.BlockSpec(memory_space=pl.ANY)],
        out_specs=tok,
        scratch_shapes=[pltpu.VMEM((2, group, page_rows, DA_V_DIM), F32),
                        pltpu.VMEM((2, group, page_rows, DA_V_DIM), F32),
                        pltpu.SemaphoreType.DMA((2, 2)),
                        pltpu.VMEM((DA_HEADS, 2 * dec_seq, DA_V_DIM), F32),
                        pltpu.VMEM((DA_HEADS, 2 * dec_seq, 1), F32),
                        pltpu.VMEM((DA_HEADS, 2 * dec_seq, 1), F32),
                        pltpu.VMEM((DA_HEADS, 2 * dec_seq, DA_V_DIM), F32),
                        pltpu.VMEM((DA_HEADS * 2 * dec_seq, page_rows), F32),
                        pltpu.VMEM((DA_HEADS * 2 * dec_seq, page_rows), F32)],
    )
    return pl.pallas_call(
        body,
        grid_spec=grid_spec,
        out_shape=jax.ShapeDtypeStruct((dec_batch * dec_seq, DA_WIDTH), F32),
        compiler_params=_params("arbitrary"),
        name="attn_sample",
    )(page_table, _alibi_slopes(), q, k_new, v_new, lam_params, subln_w.reshape(1, DA_V_DIM),
      cache_k, cache_v)


FFN_CHUNK = 256


def _ffn_body(x_ref, orw_ref, oda_ref, wo_ref, g2_ref, w1_ref, w3_ref, w2_ref, fg_ref, out_ref, *, final):
    x1 = (x_ref[...]
          + jnp.dot(orw_ref[...].astype(BF16), wo_ref[:RW_WIDTH, :], preferred_element_type=F32)
          + jnp.dot(oda_ref[...].astype(BF16), wo_ref[RW_WIDTH:, :], preferred_element_type=F32))
    hf = _rms(x1, g2_ref[...], NORM_EPS).astype(BF16)
    acc = jnp.zeros_like(x1)
    for c0 in range(0, D_FF, FFN_CHUNK):
        a = jnp.dot(hf, w1_ref[:, c0:c0 + FFN_CHUNK], preferred_element_type=F32)
        b = jnp.dot(hf, w3_ref[:, c0:c0 + FFN_CHUNK], preferred_element_type=F32)
        z = (a * jax.nn.sigmoid(a) * b).astype(BF16)
        acc = acc + jnp.dot(z, w2_ref[c0:c0 + FFN_CHUNK, :], preferred_element_type=F32)
    x2 = x1 + acc
    out_ref[...] = _rms(x2, fg_ref[...], NORM_EPS) if final else x2


def _outproj_ffn(x2d, o_rw, o_da, wo, g2, w1, w3, w2, final_g, final):
    n = x2d.shape[0]
    tm = min(512, n)
    row = lambda width: pl.BlockSpec((tm, width), lambda i: (i, 0))
    full = lambda r, c: pl.BlockSpec((r, c), lambda i: (0, 0))
    return pl.pallas_call(
        functools.partial(_ffn_body, final=final),
        grid=(n // tm,),
        in_specs=[row(D_MODEL), row(RW_WIDTH), row(DA_WIDTH), full(D_MODEL, D_MODEL), full(1, D_MODEL),
                  full(D_MODEL, D_FF), full(D_MODEL, D_FF), full(D_FF, D_MODEL), full(1, D_MODEL)],
        out_specs=row(D_MODEL),
        out_shape=jax.ShapeDtypeStruct((n, D_MODEL), F32),
        compiler_params=_params("parallel"),
        name="outproj_ffn",
    )(x2d, o_rw, o_da, wo, g2.reshape(1, D_MODEL), w1, w3, w2, final_g.reshape(1, D_MODEL))


def _lambda_init(layer):
    return 0.8 - 0.6 * math.exp(-0.3 * layer)


def kernel(x_prompt, x_sample, cache_k, cache_v, state_wkv, state_shift, page_table, norm1_g, w_in, rw_mu, rw_w0, rw_w2, rw_a0, rw_a2, rw_g2, rw_k_k, rw_k_a, rw_r_k, rw_lnx_w, rw_lnx_b, da_lam_q1, da_lam_k1, da_lam_q2, da_lam_k2, da_subln_w, w_out, norm2_g, ffn_w1, ffn_w3, ffn_w2, final_g):
    depth = w_in.shape[0]
    batch, seq, _ = x_prompt.shape
    dec_batch, dec_seq, _ = x_sample.shape
    n_pool, page_size = cache_k.shape[1], cache_k.shape[2]
    ck = cache_k.reshape(depth, n_pool, page_size * DA_HEADS, 2 * DA_HEAD_DIM)
    cv = cache_v.reshape(depth, n_pool, page_size * DA_HEADS, DA_V_DIM)

    xp = x_prompt.reshape(batch * seq, D_MODEL)
    xs = x_sample.reshape(dec_batch * dec_seq, D_MODEL)
    outs = {name: [] for name in ("sp", "shp", "ks", "vs", "ss", "shs")}
    kv_prompt = []
    for l in range(depth):
        lam_init = _lambda_init(l)
        lp = dict(mu=rw_mu[l], w0=rw_w0[l], w2=rw_w2[l], a0=rw_a0[l], a2=rw_a2[l], g2=rw_g2[l],
                  k_k=rw_k_k[l], k_a=rw_k_a[l], r_k=rw_r_k[l].reshape(RW_WIDTH),
                  lnx_w=rw_lnx_w[l], lnx_b=rw_lnx_b[l])
        lam_params = jnp.stack([da_lam_q1[l], da_lam_k1[l], da_lam_q2[l], da_lam_k2[l]])
        w_in_b = w_in[l].astype(BF16)
        wo_b, w1_b, w3_b, w2_b = (w_out[l].astype(BF16), ffn_w1[l].astype(BF16),
                                  ffn_w3[l].astype(BF16), ffn_w2[l].astype(BF16))
        final = l == depth - 1

        p_rw, k_new, v_new, q_b, k_b, vt_b = _inproj_prompt(xp, norm1_g[l], w_in_b, kv_prompt if final else [])
        kv_prompt.append((k_new, v_new))
        o_rw, s_new = _rwkv_mixer(p_rw, jnp.zeros((batch, RW_PROJ), F32),
                                  jnp.zeros((batch, RW_HEADS, RW_HEAD_DIM, RW_HEAD_DIM), F32),
                                  batch, seq, lp)
        o_da = _attn_prompt(q_b, k_b, vt_b, lam_params, da_subln_w[l], lam_init, batch, seq)
        xp = _outproj_ffn(xp, o_rw, o_da, wo_b, norm2_g[l], w1_b, w3_b, w2_b, final_g, final)
        outs["sp"].append(s_new)
        outs["shp"].append(p_rw.reshape(batch, seq, RW_PROJ)[:, -1])

        p_rw, q, k, v = _inproj(xs, norm1_g[l], w_in_b)
        o_rw, s_new = _rwkv_mixer(p_rw, state_shift[l], state_wkv[l], dec_batch, dec_seq, lp)
        o_da = _attn_sample(q, k, v, ck, cv, page_table, lam_params, da_subln_w[l], lam_init, l,
                            dec_batch, dec_seq)
        xs = _outproj_ffn(xs, o_rw, o_da, wo_b, norm2_g[l], w1_b, w3_b, w2_b, final_g, final)
        outs["ks"].append(k.reshape(dec_batch, dec_seq, DA_HEADS, 2 * DA_HEAD_DIM))
        outs["vs"].append(v.reshape(dec_batch, dec_seq, DA_HEADS, DA_V_DIM))
        outs["ss"].append(s_new)
        outs["shs"].append(p_rw.reshape(dec_batch, dec_seq, RW_PROJ)[:, -1])

    kp = kv_prompt[-1][0].reshape(depth, batch, seq, DA_HEADS, 2 * DA_HEAD_DIM)
    vp = kv_prompt[-1][1].reshape(depth, batch, seq, DA_HEADS, DA_V_DIM)
    return (xp.reshape(batch, seq, D_MODEL), xs.reshape(dec_batch, dec_seq, D_MODEL),
            kp, vp, jnp.stack(outs["sp"]), jnp.stack(outs["shp"]),
            jnp.stack(outs["ks"]), jnp.stack(outs["vs"]), jnp.stack(outs["ss"]), jnp.stack(outs["shs"]))
```

```python
import functools
import math

import jax
import jax.numpy as jnp
import numpy as np
from jax import lax
from jax.experimental import pallas as pl
from jax.experimental.pallas import tpu as pltpu

F32 = jnp.float32
BF16 = jnp.bfloat16

D_MODEL = 1024
RW_HEAD_DIM = 64
RW_WIDTH = 512
RW_HEADS = RW_WIDTH // RW_HEAD_DIM
RW_DECAY_LORA = 64
RW_AAA_LORA = 64
RW_GATE_LORA = 128
RW_LN_EPS = 64e-5
RW_PROJ = 3 * RW_WIDTH + RW_DECAY_LORA + RW_AAA_LORA + RW_GATE_LORA
DA_HEAD_DIM = 64
DA_V_DIM = 2 * DA_HEAD_DIM
DA_HEADS = 4
DA_QK = DA_HEADS * 2 * DA_HEAD_DIM
DA_WIDTH = DA_HEADS * DA_V_DIM
ALIBI_MAX_BIAS = 8.0
D_FF = 2816
NORM_EPS = 1e-5
SUBLN_EPS = 1e-5
NEG_BIG = -1e30
LOG2E = 1.4426950408889634

LANES = 128
MXU_WIDTH = 256
RW_PAIRS = RW_WIDTH // LANES
RW_CHUNK = 64
RW_GROUP = 2

VMEM_LIMIT_BYTES = 56 * 1024 * 1024

NT_DIMS = (((1,), (1,)), ((), ()))
TN_DIMS = (((0,), (0,)), ((), ()))


def _params(*semantics):
    return pltpu.CompilerParams(dimension_semantics=semantics,
                                vmem_limit_bytes=VMEM_LIMIT_BYTES)


def _rms(x, g, eps):
    return x * lax.rsqrt(jnp.mean(x * x, axis=-1, keepdims=True) + eps) * g


def _mm(a, b, dims=None):
    a, b = a.astype(BF16), b.astype(BF16)
    if dims is None:
        return jnp.dot(a, b, preferred_element_type=F32)
    return lax.dot_general(a, b, dims, preferred_element_type=F32)


def _split_terms(x, terms):
    out = []
    for _ in range(terms):
        hi = x.astype(BF16)
        out.append(hi)
        x = x - hi.astype(F32)
    return out


def _inproj_body(x_ref, g_ref, w_ref, prw_ref, q_ref, k_ref, v_ref):
    h = _rms(x_ref[...], g_ref[...], NORM_EPS).astype(BF16)
    c0, c1, c2 = RW_PROJ, RW_PROJ + DA_QK, RW_PROJ + 2 * DA_QK
    prw_ref[...] = jnp.dot(h, w_ref[:, :c0], preferred_element_type=F32)
    q_ref[...] = jnp.dot(h, w_ref[:, c0:c1], preferred_element_type=F32)
    k_ref[...] = jnp.dot(h, w_ref[:, c1:c2], preferred_element_type=F32)
    v_ref[...] = jnp.dot(h, w_ref[:, c2:], preferred_element_type=F32)


def _inproj_prompt_body(x_ref, g_ref, w_ref, wvt_ref, *rest, n_prev):
    prev = rest[:2 * n_prev]
    prw_ref, k_ref, v_ref, qb_ref, kb_ref, vtb_ref = rest[2 * n_prev:]
    if n_prev:
        for li in range(n_prev):
            k_ref[li] = prev[2 * li][...]
            v_ref[li] = prev[2 * li + 1][...]
        k_ref, v_ref = k_ref.at[n_prev], v_ref.at[n_prev]
    tm = x_ref.shape[0]
    h = _rms(x_ref[...], g_ref[...], NORM_EPS).astype(BF16)
    c0, c1, c2 = RW_PROJ, RW_PROJ + DA_QK, RW_PROJ + 2 * DA_QK
    prw_ref[...] = jnp.dot(h, w_ref[:, :c0], preferred_element_type=F32)
    q = jnp.dot(h, w_ref[:, c0:c1], preferred_element_type=F32)
    qb_ref[...] = (q * (DA_HEAD_DIM ** -0.5 * LOG2E)).astype(BF16)
    k = jnp.dot(h, w_ref[:, c1:c2], preferred_element_type=F32)
    kb_ref[...] = k.astype(BF16)
    v = jnp.dot(h, w_ref[:, c2:], preferred_element_type=F32)
    for hd in range(DA_HEADS):
        head_rows = pl.ds(hd, tm, stride=DA_HEADS)
        k_ref[head_rows, :] = k[:, hd * DA_V_DIM:(hd + 1) * DA_V_DIM]
        v_ref[head_rows, :] = v[:, hd * DA_V_DIM:(hd + 1) * DA_V_DIM]
    vtb_ref[...] = lax.dot_general(wvt_ref[...], h, NT_DIMS, preferred_element_type=F32).astype(BF16)


def _inproj_prompt(x2d, g, w_bf16, kv_prev):
    n = x2d.shape[0]
    tm = min(512, n)
    in_proj = w_bf16.shape[1]
    wvt = w_bf16[:, RW_PROJ + 2 * DA_QK:].T
    n_prev = len(kv_prev)
    row = lambda width: pl.BlockSpec((tm, width), lambda i: (i, 0))
    cache_rows = pl.BlockSpec((tm * DA_HEADS, DA_V_DIM), lambda i: (i, 0))
    if n_prev:
        cache_out = pl.BlockSpec((n_prev + 1, tm * DA_HEADS, DA_V_DIM), lambda i: (0, i, 0))
        cache_shape = jax.ShapeDtypeStruct((n_prev + 1, n * DA_HEADS, DA_V_DIM), F32)
    else:
        cache_out = cache_rows
        cache_shape = jax.ShapeDtypeStruct((n * DA_HEADS, DA_V_DIM), F32)
    return pl.pallas_call(
        functools.partial(_inproj_prompt_body, n_prev=n_prev),
        grid=(n // tm,),
        in_specs=[row(D_MODEL),
                  pl.BlockSpec((1, D_MODEL), lambda i: (0, 0)),
                  pl.BlockSpec((D_MODEL, in_proj), lambda i: (0, 0)),
                  pl.BlockSpec((DA_WIDTH, D_MODEL), lambda i: (0, 0))] + [cache_rows] * (2 * n_prev),
        out_specs=[row(RW_PROJ), cache_out, cache_out, row(DA_QK), row(DA_QK),
                   pl.BlockSpec((DA_WIDTH, tm), lambda i: (0, i))],
        out_shape=[jax.ShapeDtypeStruct((n, RW_PROJ), F32), cache_shape, cache_shape,
                   jax.ShapeDtypeStruct((n, DA_QK), BF16),
                   jax.ShapeDtypeStruct((n, DA_QK), BF16),
                   jax.ShapeDtypeStruct((DA_WIDTH, n), BF16)],
        compiler_params=_params("parallel"),
        name="inproj_prompt",
    )(x2d, g.reshape(1, D_MODEL), w_bf16, wvt, *[a for kv in kv_prev for a in kv])


def _inproj(x2d, g, w_bf16):
    n = x2d.shape[0]
    tm = min(512, n)
    in_proj = w_bf16.shape[1]
    row = lambda width: pl.BlockSpec((tm, width), lambda i: (i, 0))
    return pl.pallas_call(
        _inproj_body,
        grid=(n // tm,),
        in_specs=[row(D_MODEL),
                  pl.BlockSpec((1, D_MODEL), lambda i: (0, 0)),
                  pl.BlockSpec((D_MODEL, in_proj), lambda i: (0, 0))],
        out_specs=[row(RW_PROJ), row(DA_QK), row(DA_QK), row(DA_WIDTH)],
        out_shape=[jax.ShapeDtypeStruct((n, RW_PROJ), F32),
                   jax.ShapeDtypeStruct((n, DA_QK), F32),
                   jax.ShapeDtypeStruct((n, DA_QK), F32),
                   jax.ShapeDtypeStruct((n, DA_WIDTH), F32)],
        compiler_params=_params("parallel"),
        name="inproj",
    )(x2d, g.reshape(1, D_MODEL), w_bf16)


def _softplus(z):
    return jnp.maximum(z, 0.0) + jnp.log(1.0 + jnp.exp(-jnp.abs(z)))


def _head_sums(x, ones_heads):
    width = ones_heads.shape[0]
    cols = [jnp.dot(x[:, c0:c0 + width].astype(BF16), ones_heads, preferred_element_type=F32)
            for c0 in range(0, RW_WIDTH, width)]
    return jnp.concatenate(cols, axis=-1)


def _cat0(*xs):
    return jnp.concatenate(xs, axis=0)


def _cat1(*xs):
    return jnp.concatenate(xs, axis=1)


def _rwkv_body(p_ref, prev_ref, s0_ref, mu_ref, w0_ref, w2_ref, a0_ref, a2_ref, g2_ref,
               kk_ref, ka_ref, rk_ref, lnw_ref, lnb_ref, tril_ref, ones_ref,
               o_ref, sout_ref, carry_ref, s_ref, *, tm, valid):
    i = pl.program_id(1)
    c_len = RW_CHUNK
    half = RW_HEAD_DIM
    zeros_hh = jnp.zeros((half, half), F32)

    @pl.when(i == 0)
    def _():
        carry_ref[...] = prev_ref[0]
        for q in range(RW_PAIRS):
            top = _cat1(s0_ref[0, 2 * q], zeros_hh)
            bot = _cat1(zeros_hh, s0_ref[0, 2 * q + 1])
            s_ref[q] = _cat0(top, bot)

    p = p_ref[...]
    prev_row = carry_ref[...]
    carry_ref[...] = p[valid - 1:valid, :]
    if valid < tm:
        p = _cat0(p, jnp.zeros((tm - valid, RW_PROJ), F32))
    rolled = pltpu.roll(p, 1, axis=0)
    row = lax.broadcasted_iota(jnp.int32, (tm, 1), 0)
    prev = jnp.where(row == 0, prev_row, rolled)
    xs_all = p + (prev - p) * mu_ref[...]

    nchunks = tm // c_len
    group = min(RW_GROUP, nchunks)
    block = group * c_len
    ones_heads = ones_ref[...]
    tril = tril_ref[...]
    c1, c2, c3 = RW_WIDTH, 2 * RW_WIDTH, 3 * RW_WIDTH
    c4 = c3 + RW_DECAY_LORA
    c5 = c4 + RW_AAA_LORA

    def prepare(r0):
        xs = xs_all[r0:r0 + block]
        r, k, v = xs[:, :c1], xs[:, c1:c2], xs[:, c2:c3]
        wd, ad, gd = xs[:, c3:c4], xs[:, c4:c5], xs[:, c5:]
        w = -_softplus(-(w0_ref[...] + jnp.dot(jnp.tanh(wd), w2_ref[...], preferred_element_type=F32))) - 0.5
        lw = -jnp.exp(w)
        a = jax.nn.sigmoid(a0_ref[...] + jnp.dot(ad, a2_ref[...], preferred_element_type=F32))
        g = jnp.dot(jax.nn.sigmoid(gd), g2_ref[...], preferred_element_type=F32)
        kk = k * kk_ref[...]
        kk = kk / jnp.maximum(jnp.sqrt(_head_sums(kk * kk, ones_heads)), 1e-12)
        k_h = k * (1.0 + (a - 1.0) * ka_ref[...])
        bonus = _head_sums(r * k_h * rk_ref[...], ones_heads) * v
        b = kk * a
        if valid < tm:
            live = row[r0:r0 + block] < valid
            lw = jnp.where(live, lw, 0.0)
            kk, b, k_h, r, v = (jnp.where(live, t, 0.0) for t in (kk, b, k_h, r, v))
        cum = sum(jnp.dot(tril, t, preferred_element_type=F32) for t in _split_terms(lw, 3))
        tot = _cat0(*[jnp.broadcast_to(cum[(c + 1) * c_len - 1:(c + 1) * c_len, :], (c_len, RW_WIDTH))
                      for c in range(group)])
        e_neg = jnp.exp(-cum)
        e_rest = jnp.exp(tot - cum)
        return dict(at=-kk * jnp.exp(cum - lw), bt=b * e_neg, kt=k_h * e_neg, rt=r * jnp.exp(cum),
                    bh=b * e_rest, kh=k_h * e_rest, v=v, ptot=jnp.exp(tot), bonus=bonus, g=g)

    blocks = [prepare(r0) for r0 in range(0, tm, block)]

    lane = lax.broadcasted_iota(jnp.int32, (c_len, LANES), 1)
    first = lane < half
    rr = lax.broadcasted_iota(jnp.int32, (LANES, LANES), 0)
    cc = lax.broadcasted_iota(jnp.int32, (LANES, LANES), 1)
    strict = rr > cc
    incl = rr >= cc
    eye = (rr == cc).astype(F32)
    zeros_ll = jnp.zeros((LANES, LANES), F32)
    steps = int(math.log2(c_len))
    inv_n = 1.0 / RW_HEAD_DIM

    def stack(x):
        return _cat0(jnp.where(first, x, 0.0), jnp.where(first, 0.0, x))

    s_cur = [s_ref[q] for q in range(RW_PAIRS)]
    outs = []
    for blk in blocks:
        units = [(c, q) for c in range(group) for q in range(RW_PAIRS)]
        every = range(len(units))

        def stacks(x):
            return [stack(x[c * c_len:(c + 1) * c_len, q * LANES:(q + 1) * LANES]) for c, q in units]

        la, lr, sb, sk, sv, sbh, skh = (stacks(blk[name]) for name in ("at", "rt", "bt", "kt", "v", "bh", "kh"))
        prod = [_mm(_cat0(la[u], lr[u]), _cat0(sb[u], sk[u]), NT_DIMS) for u in every]
        a_ab = [jnp.where(strict, prod[u][:LANES, :LANES], 0.0) for u in every]
        a_ak = [jnp.where(strict, prod[u][:LANES, LANES:], 0.0) for u in every]
        a_rb = [jnp.where(incl, prod[u][LANES:, :LANES], 0.0) for u in every]
        a_rk = [jnp.where(incl, prod[u][LANES:, LANES:], 0.0) for u in every]
        av = [_mm(a_ak[u], sv[u]) for u in every]
        pw = [_mm(a_ab[u], a_ab[u]) for u in every]
        inv = [eye + a_ab[u] for u in every]
        for m in range(1, steps):
            if m == steps - 1:
                res = [_mm(pw[u], inv[u]) for u in every]
                inv = [inv[u] + res[u] for u in every]
            else:
                res = [_mm(pw[u], _cat1(pw[u], inv[u])) for u in every]
                pw = [res[u][:, :LANES] for u in every]
                inv = [inv[u] + res[u][:, LANES:] for u in every]
        wmat = [_mm(inv[u], _cat1(la[u], av[u])) for u in every]
        z = [_cat0(wmat[u], _cat1(zeros_ll, sv[u])) for u in every]
        qy = [_mm(_cat1(a_rb[u], a_rk[u]), z[u]) for u in every]
        gn = [_mm(z[u], _cat0(sbh[u], skh[u]), TN_DIMS) for u in every]
        qt = [lr[u] + qy[u][:, :LANES] for u in every]
        qt = [qt[u][:c_len] + qt[u][c_len:] for u in every]
        y0 = [qy[u][:c_len, LANES:] + qy[u][c_len:, LANES:] for u in every]
        ys = []
        for c in range(group):
            us = [c * RW_PAIRS + q for q in range(RW_PAIRS)]
            y_pairs = [y0[u] + _mm(qt[u], s_cur[q], NT_DIMS) for q, u in enumerate(us)]
            upd = [_mm(s_cur[q], gn[u][:LANES]) for q, u in enumerate(us)]
            ptot = blk["ptot"][c * c_len:c * c_len + 1]
            s_cur = [s_cur[q] * ptot[:, q * LANES:(q + 1) * LANES] + upd[q] + gn[u][LANES:]
                     for q, u in enumerate(us)]
            ys.append(_cat1(*y_pairs))
        y = _cat0(*ys) if group > 1 else ys[0]
        mean = _head_sums(y, ones_heads) * inv_n
        d = y - mean
        var = _head_sums(d * d, ones_heads) * inv_n
        y = d * lax.rsqrt(var + RW_LN_EPS) * lnw_ref[...] + lnb_ref[...]
        outs.append((y + blk["bonus"]) * blk["g"])
    for q in range(RW_PAIRS):
        s_ref[q] = s_cur[q]
    out = _cat0(*outs) if len(outs) > 1 else outs[0]
    o_ref[...] = out[:valid]

    @pl.when(i == pl.num_programs(1) - 1)
    def _():
        for q in range(RW_PAIRS):
            sout_ref[0, 2 * q] = s_cur[q][:half, :half]
            sout_ref[0, 2 * q + 1] = s_cur[q][half:, half:]


def _rwkv_mixer(p_rw, prev_rows, s0, batch, seq, lp):
    n = batch * seq
    if seq % RW_CHUNK == 0:
        tm = valid = min(256, seq)
    else:
        assert seq < RW_CHUNK and seq % 8 == 0
        tm, valid = RW_CHUNK, seq
    nt = seq // valid
    block = min(RW_GROUP * RW_CHUNK, tm)
    idx = np.arange(block)
    tril = (((idx[:, None] // RW_CHUNK) == (idx[None, :] // RW_CHUNK)) & (idx[None, :] <= idx[:, None]))
    lane = np.arange(MXU_WIDTH) // RW_HEAD_DIM
    ones_heads = lane[:, None] == lane[None, :]
    vec = lambda width: pl.BlockSpec((1, width), lambda b, i: (0, 0))
    full = lambda r, c: pl.BlockSpec((r, c), lambda b, i: (0, 0))
    state = pl.BlockSpec((1, RW_HEADS, RW_HEAD_DIM, RW_HEAD_DIM), lambda b, i: (b, 0, 0, 0))
    return pl.pallas_call(
        functools.partial(_rwkv_body, tm=tm, valid=valid),
        grid=(batch, nt),
        in_specs=[pl.BlockSpec((valid, RW_PROJ), lambda b, i: (b * nt + i, 0)),
                  pl.BlockSpec((1, 1, RW_PROJ), lambda b, i: (b, 0, 0)),
                  state,
                  vec(RW_PROJ), vec(RW_WIDTH), full(RW_DECAY_LORA, RW_WIDTH),
                  vec(RW_WIDTH), full(RW_AAA_LORA, RW_WIDTH), full(RW_GATE_LORA, RW_WIDTH),
                  vec(RW_WIDTH), vec(RW_WIDTH), vec(RW_WIDTH), vec(RW_WIDTH), vec(RW_WIDTH),
                  full(block, block), full(MXU_WIDTH, MXU_WIDTH)],
        out_specs=[pl.BlockSpec((valid, RW_WIDTH), lambda b, i: (b * nt + i, 0)), state],
        out_shape=[jax.ShapeDtypeStruct((n, RW_WIDTH), F32),
                   jax.ShapeDtypeStruct((batch, RW_HEADS, RW_HEAD_DIM, RW_HEAD_DIM), F32)],
        scratch_shapes=[pltpu.VMEM((1, RW_PROJ), F32),
                        pltpu.VMEM((RW_PAIRS, LANES, LANES), F32)],
        compiler_params=_params("parallel", "arbitrary"),
        name="rwkv",
    )(p_rw, prev_rows.reshape(batch, 1, RW_PROJ), s0,
      lp["mu"].reshape(1, RW_PROJ), lp["w0"].reshape(1, RW_WIDTH), lp["w2"],
      lp["a0"].reshape(1, RW_WIDTH), lp["a2"], lp["g2"],
      lp["k_k"].reshape(1, RW_WIDTH), lp["k_a"].reshape(1, RW_WIDTH), lp["r_k"].reshape(1, RW_WIDTH),
      lp["lnx_w"].reshape(1, RW_WIDTH), lp["lnx_b"].reshape(1, RW_WIDTH),
      jnp.asarray(tril, BF16), jnp.asarray(ones_heads, BF16))


def _lam_of(lam_ref, lam_init):
    lq1, lk1, lq2, lk2 = lam_ref[0:1, :], lam_ref[1:2, :], lam_ref[2:3, :], lam_ref[3:4, :]
    return (jnp.exp(jnp.sum(lq1 * lk1, axis=-1, keepdims=True))
            - jnp.exp(jnp.sum(lq2 * lk2, axis=-1, keepdims=True)) + lam_init)


def _diff_finish(acc1, l1, acc2, l2, lam, subln_w, lam_init):
    o = acc1 / l1 - lam * (acc2 / l2)
    o = o * lax.rsqrt(jnp.mean(o * o, axis=-1, keepdims=True) + SUBLN_EPS) * subln_w
    return o * (1.0 - lam_init)


def _attn_prompt_body(qi_ref, kj_ref, slopes_ref, q_ref, k_ref, vt_ref, lam_ref, subln_ref, o_ref,
                      m_ref, l_ref, acc_ref, bias_ref, *, lam_init, tq, tk):
    hg = pl.program_id(1)
    t = pl.program_id(2)
    i = qi_ref[t]
    j = kj_ref[t]
    group_heads = range(ATTN_HEADS_PER_STEP)
    neg_slope = [-slopes_ref[hg * ATTN_HEADS_PER_STEP + hh] * LOG2E for hh in group_heads]

    @pl.when(t == 0)
    def _():
        kpos = lax.broadcasted_iota(jnp.int32, (tk, tq), 0)
        qpos = lax.broadcasted_iota(jnp.int32, (tk, tq), 1)
        dist = (qpos - kpos).astype(F32)
        for hh in group_heads:
            bias_ref[hh] = dist * neg_slope[hh]

    @pl.when(j == 0)
    def _():
        m_ref[...] = jnp.full(m_ref.shape, NEG_BIG, F32)
        l_ref[...] = jnp.zeros(l_ref.shape, F32)
        acc_ref[...] = jnp.zeros(acc_ref.shape, F32)

    tile_dist = jnp.full((1, 1), i * tq - j * tk, jnp.int32).astype(F32)
    off = [tile_dist * neg_slope[hh] for hh in group_heads]

    def update(on_diagonal):
        q = q_ref[...]
        k = k_ref[...]
        vt = vt_ref[...]
        units = [(hh, c, q0) for hh in group_heads for c in range(2) for q0 in range(0, tq, ATTN_Q_BLOCK)]
        every = range(len(units))
        ss = []
        for hh, c, q0 in units:
            sl = slice(hh * DA_V_DIM + c * DA_HEAD_DIM, hh * DA_V_DIM + (c + 1) * DA_HEAD_DIM)
            qs = slice(q0, q0 + ATTN_Q_BLOCK)
            s = lax.dot_general(k[:, sl], q[qs, sl], NT_DIMS, preferred_element_type=F32) + bias_ref[hh, :, qs]
            if on_diagonal:
                kpos = lax.broadcasted_iota(jnp.int32, (tk, ATTN_Q_BLOCK), 0)
                qpos = lax.broadcasted_iota(jnp.int32, (tk, ATTN_Q_BLOCK), 1) + q0
                s = jnp.where(qpos >= kpos, s, NEG_BIG)
            ss.append(s)
        m_prev = [m_ref[2 * hh + c, :, q0:q0 + ATTN_Q_BLOCK] for hh, c, q0 in units]
        m_new = [jnp.maximum(m_prev[u], jnp.max(ss[u], axis=0, keepdims=True) + off[units[u][0]]) for u in every]
        ps = [jnp.exp2(ss[u] - (m_new[u] - off[units[u][0]])) for u in every]
        pv = [jnp.dot(vt[units[u][0] * DA_V_DIM:(units[u][0] + 1) * DA_V_DIM], ps[u].astype(BF16),
                      preferred_element_type=F32) for u in every]
        for u, (hh, c, q0) in enumerate(units):
            qs = slice(q0, q0 + ATTN_Q_BLOCK)
            alpha = jnp.exp2(m_prev[u] - m_new[u])
            l_ref[2 * hh + c, :, qs] = alpha * l_ref[2 * hh + c, :, qs] + jnp.sum(ps[u], axis=0, keepdims=True)
            acc_ref[2 * hh + c, :, qs] = alpha * acc_ref[2 * hh + c, :, qs] + pv[u]
            m_ref[2 * hh + c, :, qs] = m_new[u]

    @pl.when(j < i)
    def _():
        update(False)

    @pl.when(j == i)
    def _():
        update(True)
        lam = _lam_of(lam_ref, lam_init)
        for hh in group_heads:
            o = acc_ref[2 * hh] / l_ref[2 * hh] - lam * (acc_ref[2 * hh + 1] / l_ref[2 * hh + 1])
            o = o * lax.rsqrt(jnp.mean(o * o, axis=0, keepdims=True) + SUBLN_EPS)
            o_ref[:, hh * DA_V_DIM:(hh + 1) * DA_V_DIM] = o.T * subln_ref[...] * (1.0 - lam_init)


def _alibi_slopes():
    return jnp.exp2(-ALIBI_MAX_BIAS / DA_HEADS * jnp.arange(1, DA_HEADS + 1, dtype=F32))


ATTN_Q_BLOCK = 256
ATTN_HEADS_PER_STEP = 2


def _attn_prompt(q, k, vt, lam_params, subln_w, lam_init, batch, seq):
    n = batch * seq
    tq = tk = min(512, seq)
    assert tq % ATTN_Q_BLOCK == 0 and DA_HEADS % ATTN_HEADS_PER_STEP == 0
    nq = seq // tq
    hps = ATTN_HEADS_PER_STEP
    width = hps * DA_V_DIM
    pairs = [(i, j) for i in range(nq) for j in range(i + 1)]
    qi = jnp.asarray([pr[0] for pr in pairs], jnp.int32)
    kj = jnp.asarray([pr[1] for pr in pairs], jnp.int32)
    body = functools.partial(_attn_prompt_body, lam_init=lam_init, tq=tq, tk=tk)
    grid_spec = pltpu.PrefetchScalarGridSpec(
        num_scalar_prefetch=3,
        grid=(batch, DA_HEADS // hps, len(pairs)),
        in_specs=[pl.BlockSpec((tq, width), lambda b, h, t, qi, kj, sl: (b * nq + qi[t], h)),
                  pl.BlockSpec((tk, width), lambda b, h, t, qi, kj, sl: (b * nq + kj[t], h)),
                  pl.BlockSpec((width, tk), lambda b, h, t, qi, kj, sl: (h, b * nq + kj[t])),
                  pl.BlockSpec((4, DA_HEAD_DIM), lambda b, h, t, qi, kj, sl: (0, 0)),
                  pl.BlockSpec((1, DA_V_DIM), lambda b, h, t, qi, kj, sl: (0, 0))],
        out_specs=pl.BlockSpec((tq, width), lambda b, h, t, qi, kj, sl: (b * nq + qi[t], h)),
        scratch_shapes=[pltpu.VMEM((2 * hps, 1, tq), F32), pltpu.VMEM((2 * hps, 1, tq), F32),
                        pltpu.VMEM((2 * hps, DA_V_DIM, tq), F32), pltpu.VMEM((hps, tk, tq), F32)],
    )
    return pl.pallas_call(
        body,
        grid_spec=grid_spec,
        out_shape=jax.ShapeDtypeStruct((n, DA_WIDTH), F32),
        compiler_params=_params("parallel", "parallel", "arbitrary"),
        name="attn_prompt",
    )(qi, kj, _alibi_slopes(), q, k, vt, lam_params, subln_w.reshape(1, DA_V_DIM))


def _attn_sample_body(pt_ref, slopes_ref, q_ref, kn_ref, vn_ref, lam_ref, subln_ref, ck_hbm, cv_hbm, o_ref,
                      kbuf, vbuf, sem, qbd_ref, m_ref, l_ref, acc_ref, slope_ref, bias_ref,
                      *, lam_init, layer, group, page_size, n_pages, n_new, n_seqs):
    b = pl.program_id(0)
    rows2 = 2 * n_new
    n_past = n_pages * page_size
    n_groups = n_pages // group
    n_stream = n_seqs * n_groups
    scale = DA_HEAD_DIM ** -0.5
    heads = range(DA_HEADS)
    pages = range(group)

    def group_copies(n):
        seq = n // n_groups
        first_page = (n - seq * n_groups) * group
        slot = n % DECODE_SLOTS
        copies = []
        for u in pages:
            page = pt_ref[seq, first_page + u]
            copies.append(pltpu.make_async_copy(ck_hbm.at[layer, page], kbuf.at[slot, u], sem.at[0, slot]))
            copies.append(pltpu.make_async_copy(cv_hbm.at[layer, page], vbuf.at[slot, u], sem.at[1, slot]))
        return copies

    @pl.when(b == 0)
    def _():
        for n in range(min(DECODE_LOOKAHEAD, n_stream)):
            for cp in group_copies(n):
                cp.start()

    q = q_ref[...] * scale
    lane = lax.broadcasted_iota(jnp.int32, (n_new, DA_V_DIM), 1)
    for h in heads:
        qh = q[:, h * DA_V_DIM:(h + 1) * DA_V_DIM]
        top = jnp.where(lane < DA_HEAD_DIM, qh, 0.0)
        bot = jnp.where(lane >= DA_HEAD_DIM, qh, 0.0)
        qbd_ref[h] = jnp.concatenate([top, bot], axis=0)
    m_ref[...] = jnp.full(m_ref.shape, NEG_BIG, F32)
    l_ref[...] = jnp.zeros(l_ref.shape, F32)
    acc_ref[...] = jnp.zeros(acc_ref.shape, F32)

    all_rows = DA_HEADS * rows2
    page_rows = DA_HEADS * page_size

    @pl.when(b == 0)
    def _():
        r = lax.broadcasted_iota(jnp.int32, (all_rows, page_rows), 0)
        c = lax.broadcasted_iota(jnp.int32, (all_rows, page_rows), 1)
        row_head = r // rows2
        slope = jnp.zeros((all_rows, page_rows), F32)
        for h in heads:
            slope = jnp.where(row_head == h, slopes_ref[h], slope)
        dist = (n_past + r % n_new - c // DA_HEADS).astype(F32)
        slope_ref[...] = slope
        bias_ref[...] = jnp.where(c % DA_HEADS == row_head, -slope * dist, NEG_BIG)

    def group_step(g, carry):
        n = b * n_groups + g
        slot = n % DECODE_SLOTS

        @pl.when(n + DECODE_LOOKAHEAD < n_stream)
        def _():
            for cp in group_copies(n + DECODE_LOOKAHEAD):
                cp.start()

        for cp in group_copies(n):
            cp.wait()

        q_all = qbd_ref[...].reshape(all_rows, DA_V_DIM).astype(BF16)
        slope = slope_ref[...]
        ss = []
        for u in pages:
            first_key = jnp.full((1, 1), (g * group + u) * page_size, jnp.int32).astype(F32)
            s = lax.dot_general(q_all, kbuf[slot, u].astype(BF16), NT_DIMS, preferred_element_type=F32)
            ss.append(s + (bias_ref[...] + slope * first_key))
        m_prev = m_ref[...].reshape(all_rows, 1)
        m_new = jnp.maximum(m_prev, jnp.max(functools.reduce(jnp.maximum, ss), axis=-1, keepdims=True))
        ps = [jnp.exp(s - m_new) for s in ss]
        pv = None
        for u in pages:
            d = jnp.dot(ps[u].astype(BF16), vbuf[slot, u].astype(BF16), preferred_element_type=F32)
            pv = d if pv is None else pv + d
        alpha = jnp.exp(m_prev - m_new)
        l_new = alpha * l_ref[...].reshape(all_rows, 1) + jnp.sum(functools.reduce(jnp.add, ps), axis=-1,
                                                                   keepdims=True)
        acc_new = alpha * acc_ref[...].reshape(all_rows, DA_V_DIM) + pv
        l_ref[...] = l_new.reshape(DA_HEADS, rows2, 1)
        acc_ref[...] = acc_new.reshape(DA_HEADS, rows2, DA_V_DIM)
        m_ref[...] = m_new.reshape(DA_HEADS, rows2, 1)
        return carry

    lax.fori_loop(0, n_groups, group_step, 0)

    lam = _lam_of(lam_ref, lam_init)
    qn = lax.broadcasted_iota(jnp.int32, (rows2, n_new), 0) % n_new
    kn = lax.broadcasted_iota(jnp.int32, (rows2, n_new), 1)
    dist_new = qn - kn
    for h in heads:
        hs = slice(h * DA_V_DIM, (h + 1) * DA_V_DIM)
        s = lax.dot_general(qbd_ref[h], kn_ref[:, hs], NT_DIMS, preferred_element_type=F32)
        s = s - slopes_ref[h] * dist_new.astype(F32)
        s = jnp.where(dist_new >= 0, s, NEG_BIG)
        m_prev = m_ref[h]
        m_new = jnp.maximum(m_prev, jnp.max(s, axis=-1, keepdims=True))
        alpha = jnp.exp(m_prev - m_new)
        p = jnp.exp(s - m_new)
        l = alpha * l_ref[h] + jnp.sum(p, axis=-1, keepdims=True)
        acc = alpha * acc_ref[h] + jnp.dot(p, vn_ref[:, hs], preferred_element_type=F32)
        o_ref[:, hs] = _diff_finish(acc[:n_new], l[:n_new], acc[n_new:], l[n_new:], lam,
                                    subln_ref[...], lam_init)


DECODE_PAGE_GROUP = 8
DECODE_LOOKAHEAD = 2
DECODE_SLOTS = DECODE_LOOKAHEAD + 1


def _attn_sample(q, k_new, v_new, cache_k, cache_v, page_table, lam_params, subln_w, lam_init, layer,
                 dec_batch, dec_seq):
    page_rows = cache_k.shape[2]
    page_size = page_rows // DA_HEADS
    n_pages = page_table.shape[1]
    group = math.gcd(DECODE_PAGE_GROUP, n_pages)
    body = functools.partial(_attn_sample_body, lam_init=lam_init, layer=layer, group=group,
                             page_size=page_size, n_pages=n_pages, n_new=dec_seq, n_seqs=dec_batch)
    tok = pl.BlockSpec((dec_seq, DA_QK), lambda b, pt, sl: (b, 0))
    grid_spec = pltpu.PrefetchScalarGridSpec(
        num_scalar_prefetch=2,
        grid=(dec_batch,),
        in_specs=[tok, tok, tok,
                  pl.BlockSpec((4, DA_HEAD_DIM), lambda b, pt, sl: (0, 0)),
                  pl.BlockSpec((1, DA_V_DIM), lambda b, pt, sl: (0, 0)),
                  pl.BlockSpec(memory_space=pl.ANY),
                  pl.BlockSpec(memory_space=pl.ANY)],
        out_specs=tok,
        scratch_shapes=[pltpu.VMEM((DECODE_SLOTS, group, page_rows, DA_V_DIM), F32),
                        pltpu.VMEM((DECODE_SLOTS, group, page_rows, DA_V_DIM), F32),
                        pltpu.SemaphoreType.DMA((2, DECODE_SLOTS)),
                        pltpu.VMEM((DA_HEADS, 2 * dec_seq, DA_V_DIM), F32),
                        pltpu.VMEM((DA_HEADS, 2 * dec_seq, 1), F32),
                        pltpu.VMEM((DA_HEADS, 2 * dec_seq, 1), F32),
                        pltpu.VMEM((DA_HEADS, 2 * dec_seq, DA_V_DIM), F32),
                        pltpu.VMEM((DA_HEADS * 2 * dec_seq, page_rows), F32),
                        pltpu.VMEM((DA_HEADS * 2 * dec_seq, page_rows), F32)],
    )
    return pl.pallas_call(
        body,
        grid_spec=grid_spec,
        out_shape=jax.ShapeDtypeStruct((dec_batch * dec_seq, DA_WIDTH), F32),
        compiler_params=_params("arbitrary"),
        name="attn_sample",
    )(page_table, _alibi_slopes(), q, k_new, v_new, lam_params, subln_w.reshape(1, DA_V_DIM),
      cache_k, cache_v)


FFN_CHUNK = 256


def _ffn_body(x_ref, orw_ref, oda_ref, wo_ref, g2_ref, w1_ref, w3_ref, w2_ref, fg_ref, out_ref, *, final):
    x1 = (x_ref[...]
          + jnp.dot(orw_ref[...].astype(BF16), wo_ref[:RW_WIDTH, :], preferred_element_type=F32)
          + jnp.dot(oda_ref[...].astype(BF16), wo_ref[RW_WIDTH:, :], preferred_element_type=F32))
    hf = _rms(x1, g2_ref[...], NORM_EPS).astype(BF16)
    acc = jnp.zeros_like(x1)
    for c0 in range(0, D_FF, FFN_CHUNK):
        a = jnp.dot(hf, w1_ref[:, c0:c0 + FFN_CHUNK], preferred_element_type=F32)
        b = jnp.dot(hf, w3_ref[:, c0:c0 + FFN_CHUNK], preferred_element_type=F32)
        z = (a * jax.nn.sigmoid(a) * b).astype(BF16)
        acc = acc + jnp.dot(z, w2_ref[c0:c0 + FFN_CHUNK, :], preferred_element_type=F32)
    x2 = x1 + acc
    out_ref[...] = _rms(x2, fg_ref[...], NORM_EPS) if final else x2


def _outproj_ffn(x2d, o_rw, o_da, wo, g2, w1, w3, w2, final_g, final):
    n = x2d.shape[0]
    tm = min(512, n)
    row = lambda width: pl.BlockSpec((tm, width), lambda i: (i, 0))
    full = lambda r, c: pl.BlockSpec((r, c), lambda i: (0, 0))
    return pl.pallas_call(
        functools.partial(_ffn_body, final=final),
        grid=(n // tm,),
        in_specs=[row(D_MODEL), row(RW_WIDTH), row(DA_WIDTH), full(D_MODEL, D_MODEL), full(1, D_MODEL),
                  full(D_MODEL, D_FF), full(D_MODEL, D_FF), full(D_FF, D_MODEL), full(1, D_MODEL)],
        out_specs=row(D_MODEL),
        out_shape=jax.ShapeDtypeStruct((n, D_MODEL), F32),
        compiler_params=_params("parallel"),
        name="outproj_ffn",
    )(x2d, o_rw, o_da, wo, g2.reshape(1, D_MODEL), w1, w3, w2, final_g.reshape(1, D_MODEL))


def _lambda_init(layer):
    return 0.8 - 0.6 * math.exp(-0.3 * layer)


def kernel(x_prompt, x_sample, cache_k, cache_v, state_wkv, state_shift, page_table, norm1_g, w_in, rw_mu, rw_w0, rw_w2, rw_a0, rw_a2, rw_g2, rw_k_k, rw_k_a, rw_r_k, rw_lnx_w, rw_lnx_b, da_lam_q1, da_lam_k1, da_lam_q2, da_lam_k2, da_subln_w, w_out, norm2_g, ffn_w1, ffn_w3, ffn_w2, final_g):
    depth = w_in.shape[0]
    batch, seq, _ = x_prompt.shape
    dec_batch, dec_seq, _ = x_sample.shape
    n_pool, page_size = cache_k.shape[1], cache_k.shape[2]
    ck = cache_k.reshape(depth, n_pool, page_size * DA_HEADS, 2 * DA_HEAD_DIM)
    cv = cache_v.reshape(depth, n_pool, page_size * DA_HEADS, DA_V_DIM)

    xp = x_prompt.reshape(batch * seq, D_MODEL)
    xs = x_sample.reshape(dec_batch * dec_seq, D_MODEL)
    outs = {name: [] for name in ("sp", "shp", "ks", "vs", "ss", "shs")}
    kv_prompt = []
    for l in range(depth):
        lam_init = _lambda_init(l)
        lp = dict(mu=rw_mu[l], w0=rw_w0[l], w2=rw_w2[l], a0=rw_a0[l], a2=rw_a2[l], g2=rw_g2[l],
                  k_k=rw_k_k[l], k_a=rw_k_a[l], r_k=rw_r_k[l].reshape(RW_WIDTH),
                  lnx_w=rw_lnx_w[l], lnx_b=rw_lnx_b[l])
        lam_params = jnp.stack([da_lam_q1[l], da_lam_k1[l], da_lam_q2[l], da_lam_k2[l]])
        w_in_b = w_in[l].astype(BF16)
        wo_b, w1_b, w3_b, w2_b = (w_out[l].astype(BF16), ffn_w1[l].astype(BF16),
                                  ffn_w3[l].astype(BF16), ffn_w2[l].astype(BF16))
        final = l == depth - 1

        p_rw, k_new, v_new, q_b, k_b, vt_b = _inproj_prompt(xp, norm1_g[l], w_in_b, kv_prompt if final else [])
        kv_prompt.append((k_new, v_new))
        o_rw, s_new = _rwkv_mixer(p_rw, jnp.zeros((batch, RW_PROJ), F32),
                                  jnp.zeros((batch, RW_HEADS, RW_HEAD_DIM, RW_HEAD_DIM), F32),
                                  batch, seq, lp)
        o_da = _attn_prompt(q_b, k_b, vt_b, lam_params, da_subln_w[l], lam_init, batch, seq)
        xp = _outproj_ffn(xp, o_rw, o_da, wo_b, norm2_g[l], w1_b, w3_b, w2_b, final_g, final)
        outs["sp"].append(s_new)
        outs["shp"].append(p_rw.reshape(batch, seq, RW_PROJ)[:, -1])

        p_rw, q, k, v = _inproj(xs, norm1_g[l], w_in_b)
        o_rw, s_new = _rwkv_mixer(p_rw, state_shift[l], state_wkv[l], dec_batch, dec_seq, lp)
        o_da = _attn_sample(q, k, v, ck, cv, page_table, lam_params, da_subln_w[l], lam_init, l,
                            dec_batch, dec_seq)
        xs = _outproj_ffn(xs, o_rw, o_da, wo_b, norm2_g[l], w1_b, w3_b, w2_b, final_g, final)
        outs["ks"].append(k.reshape(dec_batch, dec_seq, DA_HEADS, 2 * DA_HEAD_DIM))
        outs["vs"].append(v.reshape(dec_batch, dec_seq, DA_HEADS, DA_V_DIM))
        outs["ss"].append(s_new)
        outs["shs"].append(p_rw.reshape(dec_batch, dec_seq, RW_PROJ)[:, -1])

    kp = kv_prompt[-1][0].reshape(depth, batch, seq, DA_HEADS, 2 * DA_HEAD_DIM)
    vp = kv_prompt[-1][1].reshape(depth, batch, seq, DA_HEADS, DA_V_DIM)
    return (xp.reshape(batch, seq, D_MODEL), xs.reshape(dec_batch, dec_seq, D_MODEL),
            kp, vp, jnp.stack(outs["sp"]), jnp.stack(outs["shp"]),
            jnp.stack(outs["ks"]), jnp.stack(outs["vs"]), jnp.stack(outs["ss"]), jnp.stack(outs["shs"]))
```

```python
import functools
import math

import jax
import jax.numpy as jnp
import numpy as np
from jax import lax
from jax.experimental import pallas as pl
from jax.experimental.pallas import tpu as pltpu

F32 = jnp.float32
BF16 = jnp.bfloat16

D_MODEL = 1024
RW_HEAD_DIM = 64
RW_WIDTH = 512
RW_HEADS = RW_WIDTH // RW_HEAD_DIM
RW_DECAY_LORA = 64
RW_AAA_LORA = 64
RW_GATE_LORA = 128
RW_LN_EPS = 64e-5
RW_PROJ = 3 * RW_WIDTH + RW_DECAY_LORA + RW_AAA_LORA + RW_GATE_LORA
DA_HEAD_DIM = 64
DA_V_DIM = 2 * DA_HEAD_DIM
DA_HEADS = 4
DA_QK = DA_HEADS * 2 * DA_HEAD_DIM
DA_WIDTH = DA_HEADS * DA_V_DIM
ALIBI_MAX_BIAS = 8.0
D_FF = 2816
NORM_EPS = 1e-5
SUBLN_EPS = 1e-5
NEG_BIG = -1e30
LOG2E = 1.4426950408889634

LANES = 128
MXU_WIDTH = 256
RW_PAIRS = RW_WIDTH // LANES
RW_CHUNK = 64
RW_GROUP = 2

VMEM_LIMIT_BYTES = 56 * 1024 * 1024

NT_DIMS = (((1,), (1,)), ((), ()))
TN_DIMS = (((0,), (0,)), ((), ()))


def _params(*semantics):
    return pltpu.CompilerParams(dimension_semantics=semantics,
                                vmem_limit_bytes=VMEM_LIMIT_BYTES)


def _rms(x, g, eps):
    return x * lax.rsqrt(jnp.mean(x * x, axis=-1, keepdims=True) + eps) * g


def _mm(a, b, dims=None):
    a, b = a.astype(BF16), b.astype(BF16)
    if dims is None:
        return jnp.dot(a, b, preferred_element_type=F32)
    return lax.dot_general(a, b, dims, preferred_element_type=F32)


def _split_terms(x, terms):
    out = []
    for _ in range(terms):
        hi = x.astype(BF16)
        out.append(hi)
        x = x - hi.astype(F32)
    return out


def _inproj_body(x_ref, g_ref, w_ref, prw_ref, q_ref, k_ref, v_ref):
    h = _rms(x_ref[...], g_ref[...], NORM_EPS).astype(BF16)
    c0, c1, c2 = RW_PROJ, RW_PROJ + DA_QK, RW_PROJ + 2 * DA_QK
    prw_ref[...] = jnp.dot(h, w_ref[:, :c0], preferred_element_type=F32)
    q_ref[...] = jnp.dot(h, w_ref[:, c0:c1], preferred_element_type=F32)
    k_ref[...] = jnp.dot(h, w_ref[:, c1:c2], preferred_element_type=F32)
    v_ref[...] = jnp.dot(h, w_ref[:, c2:], preferred_element_type=F32)


def _inproj_prompt_body(x_ref, g_ref, w_ref, wvt_ref, *rest, n_prev):
    prev = rest[:2 * n_prev]
    prw_ref, k_ref, v_ref, qb_ref, kb_ref, vtb_ref = rest[2 * n_prev:]
    if n_prev:
        for li in range(n_prev):
            k_ref[li] = prev[2 * li][...]
            v_ref[li] = prev[2 * li + 1][...]
        k_ref, v_ref = k_ref.at[n_prev], v_ref.at[n_prev]
    tm = x_ref.shape[0]
    h = _rms(x_ref[...], g_ref[...], NORM_EPS).astype(BF16)
    c0, c1, c2 = RW_PROJ, RW_PROJ + DA_QK, RW_PROJ + 2 * DA_QK
    prw_ref[...] = jnp.dot(h, w_ref[:, :c0], preferred_element_type=F32)
    q = jnp.dot(h, w_ref[:, c0:c1], preferred_element_type=F32)
    qb_ref[...] = (q * (DA_HEAD_DIM ** -0.5 * LOG2E)).astype(BF16)
    k = jnp.dot(h, w_ref[:, c1:c2], preferred_element_type=F32)
    kb_ref[...] = k.astype(BF16)
    v = jnp.dot(h, w_ref[:, c2:], preferred_element_type=F32)
    for hd in range(DA_HEADS):
        head_rows = pl.ds(hd, tm, stride=DA_HEADS)
        k_ref[head_rows, :] = k[:, hd * DA_V_DIM:(hd + 1) * DA_V_DIM]
        v_ref[head_rows, :] = v[:, hd * DA_V_DIM:(hd + 1) * DA_V_DIM]
    vtb_ref[...] = lax.dot_general(wvt_ref[...], h, NT_DIMS, preferred_element_type=F32).astype(BF16)


def _inproj_prompt(x2d, g, w_bf16, kv_prev):
    n = x2d.shape[0]
    tm = min(512, n)
    in_proj = w_bf16.shape[1]
    wvt = w_bf16[:, RW_PROJ + 2 * DA_QK:].T
    n_prev = len(kv_prev)
    row = lambda width: pl.BlockSpec((tm, width), lambda i: (i, 0))
    cache_rows = pl.BlockSpec((tm * DA_HEADS, DA_V_DIM), lambda i: (i, 0))
    if n_prev:
        cache_out = pl.BlockSpec((n_prev + 1, tm * DA_HEADS, DA_V_DIM), lambda i: (0, i, 0))
        cache_shape = jax.ShapeDtypeStruct((n_prev + 1, n * DA_HEADS, DA_V_DIM), F32)
    else:
        cache_out = cache_rows
        cache_shape = jax.ShapeDtypeStruct((n * DA_HEADS, DA_V_DIM), F32)
    return pl.pallas_call(
        functools.partial(_inproj_prompt_body, n_prev=n_prev),
        grid=(n // tm,),
        in_specs=[row(D_MODEL),
                  pl.BlockSpec((1, D_MODEL), lambda i: (0, 0)),
                  pl.BlockSpec((D_MODEL, in_proj), lambda i: (0, 0)),
                  pl.BlockSpec((DA_WIDTH, D_MODEL), lambda i: (0, 0))] + [cache_rows] * (2 * n_prev),
        out_specs=[row(RW_PROJ), cache_out, cache_out, row(DA_QK), row(DA_QK),
                   pl.BlockSpec((DA_WIDTH, tm), lambda i: (0, i))],
        out_shape=[jax.ShapeDtypeStruct((n, RW_PROJ), F32), cache_shape, cache_shape,
                   jax.ShapeDtypeStruct((n, DA_QK), BF16),
                   jax.ShapeDtypeStruct((n, DA_QK), BF16),
                   jax.ShapeDtypeStruct((DA_WIDTH, n), BF16)],
        compiler_params=_params("parallel"),
        name="inproj_prompt",
    )(x2d, g.reshape(1, D_MODEL), w_bf16, wvt, *[a for kv in kv_prev for a in kv])


def _inproj(x2d, g, w_bf16):
    n = x2d.shape[0]
    tm = min(512, n)
    in_proj = w_bf16.shape[1]
    row = lambda width: pl.BlockSpec((tm, width), lambda i: (i, 0))
    return pl.pallas_call(
        _inproj_body,
        grid=(n // tm,),
        in_specs=[row(D_MODEL),
                  pl.BlockSpec((1, D_MODEL), lambda i: (0, 0)),
                  pl.BlockSpec((D_MODEL, in_proj), lambda i: (0, 0))],
        out_specs=[row(RW_PROJ), row(DA_QK), row(DA_QK), row(DA_WIDTH)],
        out_shape=[jax.ShapeDtypeStruct((n, RW_PROJ), F32),
                   jax.ShapeDtypeStruct((n, DA_QK), F32),
                   jax.ShapeDtypeStruct((n, DA_QK), F32),
                   jax.ShapeDtypeStruct((n, DA_WIDTH), F32)],
        compiler_params=_params("parallel"),
        name="inproj",
    )(x2d, g.reshape(1, D_MODEL), w_bf16)


def _softplus(z):
    return jnp.maximum(z, 0.0) + jnp.log(1.0 + jnp.exp(-jnp.abs(z)))


def _head_sums(x, ones_heads):
    width = ones_heads.shape[0]
    cols = [jnp.dot(x[:, c0:c0 + width].astype(BF16), ones_heads, preferred_element_type=F32)
            for c0 in range(0, RW_WIDTH, width)]
    return jnp.concatenate(cols, axis=-1)


def _cat0(*xs):
    return jnp.concatenate(xs, axis=0)


def _cat1(*xs):
    return jnp.concatenate(xs, axis=1)


def _rwkv_body(p_ref, prev_ref, s0_ref, mu_ref, w0_ref, w2_ref, a0_ref, a2_ref, g2_ref,
               kk_ref, ka_ref, rk_ref, lnw_ref, lnb_ref, tril_ref, ones_ref,
               o_ref, sout_ref, carry_ref, s_ref, *, tm, valid):
    i = pl.program_id(1)
    c_len = RW_CHUNK
    half = RW_HEAD_DIM
    zeros_hh = jnp.zeros((half, half), F32)

    @pl.when(i == 0)
    def _():
        carry_ref[...] = prev_ref[0]
        for q in range(RW_PAIRS):
            top = _cat1(s0_ref[0, 2 * q], zeros_hh)
            bot = _cat1(zeros_hh, s0_ref[0, 2 * q + 1])
            s_ref[q] = _cat0(top, bot)

    p = p_ref[...]
    prev_row = carry_ref[...]
    carry_ref[...] = p[valid - 1:valid, :]
    if valid < tm:
        p = _cat0(p, jnp.zeros((tm - valid, RW_PROJ), F32))
    rolled = pltpu.roll(p, 1, axis=0)
    row = lax.broadcasted_iota(jnp.int32, (tm, 1), 0)
    prev = jnp.where(row == 0, prev_row, rolled)
    xs_all = p + (prev - p) * mu_ref[...]

    nchunks = tm // c_len
    group = min(RW_GROUP, nchunks)
    block = group * c_len
    ones_heads = ones_ref[...]
    tril = tril_ref[...]
    c1, c2, c3 = RW_WIDTH, 2 * RW_WIDTH, 3 * RW_WIDTH
    c4 = c3 + RW_DECAY_LORA
    c5 = c4 + RW_AAA_LORA

    def prepare(r0):
        xs = xs_all[r0:r0 + block]
        r, k, v = xs[:, :c1], xs[:, c1:c2], xs[:, c2:c3]
        wd, ad, gd = xs[:, c3:c4], xs[:, c4:c5], xs[:, c5:]
        w = -_softplus(-(w0_ref[...] + jnp.dot(jnp.tanh(wd), w2_ref[...], preferred_element_type=F32))) - 0.5
        lw = -jnp.exp(w)
        a = jax.nn.sigmoid(a0_ref[...] + jnp.dot(ad, a2_ref[...], preferred_element_type=F32))
        g = jnp.dot(jax.nn.sigmoid(gd), g2_ref[...], preferred_element_type=F32)
        kk = k * kk_ref[...]
        kk = kk / jnp.maximum(jnp.sqrt(_head_sums(kk * kk, ones_heads)), 1e-12)
        k_h = k * (1.0 + (a - 1.0) * ka_ref[...])
        bonus = _head_sums(r * k_h * rk_ref[...], ones_heads) * v
        b = kk * a
        if valid < tm:
            live = row[r0:r0 + block] < valid
            lw = jnp.where(live, lw, 0.0)
            kk, b, k_h, r, v = (jnp.where(live, t, 0.0) for t in (kk, b, k_h, r, v))
        cum = sum(jnp.dot(tril, t, preferred_element_type=F32) for t in _split_terms(lw, 3))
        tot = _cat0(*[jnp.broadcast_to(cum[(c + 1) * c_len - 1:(c + 1) * c_len, :], (c_len, RW_WIDTH))
                      for c in range(group)])
        e_neg = jnp.exp(-cum)
        e_rest = jnp.exp(tot - cum)
        return dict(at=-kk * jnp.exp(cum - lw), bt=b * e_neg, kt=k_h * e_neg, rt=r * jnp.exp(cum),
                    bh=b * e_rest, kh=k_h * e_rest, v=v, ptot=jnp.exp(tot), bonus=bonus, g=g)

    blocks = [prepare(r0) for r0 in range(0, tm, block)]

    lane = lax.broadcasted_iota(jnp.int32, (c_len, LANES), 1)
    first = lane < half
    rr = lax.broadcasted_iota(jnp.int32, (LANES, LANES), 0)
    cc = lax.broadcasted_iota(jnp.int32, (LANES, LANES), 1)
    strict = rr > cc
    incl = rr >= cc
    eye = (rr == cc).astype(F32)
    zeros_ll = jnp.zeros((LANES, LANES), F32)
    steps = int(math.log2(c_len))
    inv_n = 1.0 / RW_HEAD_DIM

    def stack(x):
        return _cat0(jnp.where(first, x, 0.0), jnp.where(first, 0.0, x))

    s_cur = [s_ref[q] for q in range(RW_PAIRS)]
    outs = []
    for blk in blocks:
        units = [(c, q) for c in range(group) for q in range(RW_PAIRS)]
        every = range(len(units))

        def stacks(x):
            return [stack(x[c * c_len:(c + 1) * c_len, q * LANES:(q + 1) * LANES]) for c, q in units]

        la, lr, sb, sk, sv, sbh, skh = (stacks(blk[name]) for name in ("at", "rt", "bt", "kt", "v", "bh", "kh"))
        prod = [_mm(_cat0(la[u], lr[u]), _cat0(sb[u], sk[u]), NT_DIMS) for u in every]
        a_ab = [jnp.where(strict, prod[u][:LANES, :LANES], 0.0) for u in every]
        a_ak = [jnp.where(strict, prod[u][:LANES, LANES:], 0.0) for u in every]
        a_rb = [jnp.where(incl, prod[u][LANES:, :LANES], 0.0) for u in every]
        a_rk = [jnp.where(incl, prod[u][LANES:, LANES:], 0.0) for u in every]
        av = [_mm(a_ak[u], sv[u]) for u in every]
        pw = [_mm(a_ab[u], a_ab[u]) for u in every]
        inv = [eye + a_ab[u] for u in every]
        for m in range(1, steps):
            if m == steps - 1:
                res = [_mm(pw[u], inv[u]) for u in every]
                inv = [inv[u] + res[u] for u in every]
            else:
                res = [_mm(pw[u], _cat1(pw[u], inv[u])) for u in every]
                pw = [res[u][:, :LANES] for u in every]
                inv = [inv[u] + res[u][:, LANES:] for u in every]
        wmat = [_mm(inv[u], _cat1(la[u], av[u])) for u in every]
        z = [_cat0(wmat[u], _cat1(zeros_ll, sv[u])) for u in every]
        qy = [_mm(_cat1(a_rb[u], a_rk[u]), z[u]) for u in every]
        gn = [_mm(z[u], _cat0(sbh[u], skh[u]), TN_DIMS) for u in every]
        qt = [lr[u] + qy[u][:, :LANES] for u in every]
        qt = [qt[u][:c_len] + qt[u][c_len:] for u in every]
        y0 = [qy[u][:c_len, LANES:] + qy[u][c_len:, LANES:] for u in every]
        ys = []
        for c in range(group):
            us = [c * RW_PAIRS + q for q in range(RW_PAIRS)]
            y_pairs = [y0[u] + _mm(qt[u], s_cur[q], NT_DIMS) for q, u in enumerate(us)]
            upd = [_mm(s_cur[q], gn[u][:LANES]) for q, u in enumerate(us)]
            ptot = blk["ptot"][c * c_len:c * c_len + 1]
            s_cur = [s_cur[q] * ptot[:, q * LANES:(q + 1) * LANES] + upd[q] + gn[u][LANES:]
                     for q, u in enumerate(us)]
            ys.append(_cat1(*y_pairs))
        y = _cat0(*ys) if group > 1 else ys[0]
        mean = _head_sums(y, ones_heads) * inv_n
        d = y - mean
        var = _head_sums(d * d, ones_heads) * inv_n
        y = d * lax.rsqrt(var + RW_LN_EPS) * lnw_ref[...] + lnb_ref[...]
        outs.append((y + blk["bonus"]) * blk["g"])
    for q in range(RW_PAIRS):
        s_ref[q] = s_cur[q]
    out = _cat0(*outs) if len(outs) > 1 else outs[0]
    o_ref[...] = out[:valid]

    @pl.when(i == pl.num_programs(1) - 1)
    def _():
        for q in range(RW_PAIRS):
            sout_ref[0, 2 * q] = s_cur[q][:half, :half]
            sout_ref[0, 2 * q + 1] = s_cur[q][half:, half:]


def _rwkv_mixer(p_rw, prev_rows, s0, batch, seq, lp):
    n = batch * seq
    if seq % RW_CHUNK == 0:
        tm = valid = min(256, seq)
    else:
        assert seq < RW_CHUNK and seq % 8 == 0
        tm, valid = RW_CHUNK, seq
    nt = seq // valid
    block = min(RW_GROUP * RW_CHUNK, tm)
    idx = np.arange(block)
    tril = (((idx[:, None] // RW_CHUNK) == (idx[None, :] // RW_CHUNK)) & (idx[None, :] <= idx[:, None]))
    lane = np.arange(MXU_WIDTH) // RW_HEAD_DIM
    ones_heads = lane[:, None] == lane[None, :]
    vec = lambda width: pl.BlockSpec((1, width), lambda b, i: (0, 0))
    full = lambda r, c: pl.BlockSpec((r, c), lambda b, i: (0, 0))
    state = pl.BlockSpec((1, RW_HEADS, RW_HEAD_DIM, RW_HEAD_DIM), lambda b, i: (b, 0, 0, 0))
    return pl.pallas_call(
        functools.partial(_rwkv_body, tm=tm, valid=valid),
        grid=(batch, nt),
        in_specs=[pl.BlockSpec((valid, RW_PROJ), lambda b, i: (b * nt + i, 0)),
                  pl.BlockSpec((1, 1, RW_PROJ), lambda b, i: (b, 0, 0)),
                  state,
                  vec(RW_PROJ), vec(RW_WIDTH), full(RW_DECAY_LORA, RW_WIDTH),
                  vec(RW_WIDTH), full(RW_AAA_LORA, RW_WIDTH), full(RW_GATE_LORA, RW_WIDTH),
                  vec(RW_WIDTH), vec(RW_WIDTH), vec(RW_WIDTH), vec(RW_WIDTH), vec(RW_WIDTH),
                  full(block, block), full(MXU_WIDTH, MXU_WIDTH)],
        out_specs=[pl.BlockSpec((valid, RW_WIDTH), lambda b, i: (b * nt + i, 0)), state],
        out_shape=[jax.ShapeDtypeStruct((n, RW_WIDTH), F32),
                   jax.ShapeDtypeStruct((batch, RW_HEADS, RW_HEAD_DIM, RW_HEAD_DIM), F32)],
        scratch_shapes=[pltpu.VMEM((1, RW_PROJ), F32),
                        pltpu.VMEM((RW_PAIRS, LANES, LANES), F32)],
        compiler_params=_params("parallel", "arbitrary"),
        name="rwkv",
    )(p_rw, prev_rows.reshape(batch, 1, RW_PROJ), s0,
      lp["mu"].reshape(1, RW_PROJ), lp["w0"].reshape(1, RW_WIDTH), lp["w2"],
      lp["a0"].reshape(1, RW_WIDTH), lp["a2"], lp["g2"],
      lp["k_k"].reshape(1, RW_WIDTH), lp["k_a"].reshape(1, RW_WIDTH), lp["r_k"].reshape(1, RW_WIDTH),
      lp["lnx_w"].reshape(1, RW_WIDTH), lp["lnx_b"].reshape(1, RW_WIDTH),
      jnp.asarray(tril, BF16), jnp.asarray(ones_heads, BF16))


def _lam_of(lam_ref, lam_init):
    lq1, lk1, lq2, lk2 = lam_ref[0:1, :], lam_ref[1:2, :], lam_ref[2:3, :], lam_ref[3:4, :]
    return (jnp.exp(jnp.sum(lq1 * lk1, axis=-1, keepdims=True))
            - jnp.exp(jnp.sum(lq2 * lk2, axis=-1, keepdims=True)) + lam_init)


def _diff_finish(acc1, l1, acc2, l2, lam, subln_w, lam_init):
    o = acc1 / l1 - lam * (acc2 / l2)
    o = o * lax.rsqrt(jnp.mean(o * o, axis=-1, keepdims=True) + SUBLN_EPS) * subln_w
    return o * (1.0 - lam_init)


def _attn_prompt_body(qi_ref, kj_ref, slopes_ref, q_ref, k_ref, vt_ref, lam_ref, subln_ref, o_ref,
                      m_ref, l_ref, acc_ref, bias_ref, *, lam_init, tq, tk):
    hg = pl.program_id(1)
    t = pl.program_id(2)
    i = qi_ref[t]
    j = kj_ref[t]
    group_heads = range(ATTN_HEADS_PER_STEP)
    neg_slope = [-slopes_ref[hg * ATTN_HEADS_PER_STEP + hh] * LOG2E for hh in group_heads]

    @pl.when(t == 0)
    def _():
        kpos = lax.broadcasted_iota(jnp.int32, (tk, tq), 0)
        qpos = lax.broadcasted_iota(jnp.int32, (tk, tq), 1)
        dist = (qpos - kpos).astype(F32)
        for hh in group_heads:
            bias_ref[hh] = dist * neg_slope[hh]

    @pl.when(j == 0)
    def _():
        m_ref[...] = jnp.full(m_ref.shape, NEG_BIG, F32)
        l_ref[...] = jnp.zeros(l_ref.shape, F32)
        acc_ref[...] = jnp.zeros(acc_ref.shape, F32)

    tile_dist = jnp.full((1, 1), i * tq - j * tk, jnp.int32).astype(F32)
    off = [tile_dist * neg_slope[hh] for hh in group_heads]

    def update(on_diagonal):
        q = q_ref[...]
        k = k_ref[...]
        vt = vt_ref[...]
        units = [(hh, c, q0) for hh in group_heads for c in range(2) for q0 in range(0, tq, ATTN_Q_BLOCK)]
        every = range(len(units))
        ss = []
        for hh, c, q0 in units:
            sl = slice(hh * DA_V_DIM + c * DA_HEAD_DIM, hh * DA_V_DIM + (c + 1) * DA_HEAD_DIM)
            qs = slice(q0, q0 + ATTN_Q_BLOCK)
            s = lax.dot_general(k[:, sl], q[qs, sl], NT_DIMS, preferred_element_type=F32) + bias_ref[hh, :, qs]
            if on_diagonal:
                kpos = lax.broadcasted_iota(jnp.int32, (tk, ATTN_Q_BLOCK), 0)
                qpos = lax.broadcasted_iota(jnp.int32, (tk, ATTN_Q_BLOCK), 1) + q0
                s = jnp.where(qpos >= kpos, s, NEG_BIG)
            ss.append(s)
        m_prev = [m_ref[2 * hh + c, :, q0:q0 + ATTN_Q_BLOCK] for hh, c, q0 in units]
        m_new = [jnp.maximum(m_prev[u], jnp.max(ss[u], axis=0, keepdims=True) + off[units[u][0]]) for u in every]
        ps = [jnp.exp2(ss[u] - (m_new[u] - off[units[u][0]])) for u in every]
        pv = [jnp.dot(vt[units[u][0] * DA_V_DIM:(units[u][0] + 1) * DA_V_DIM], ps[u].astype(BF16),
                      preferred_element_type=F32) for u in every]
        for u, (hh, c, q0) in enumerate(units):
            qs = slice(q0, q0 + ATTN_Q_BLOCK)
            alpha = jnp.exp2(m_prev[u] - m_new[u])
            l_ref[2 * hh + c, :, qs] = alpha * l_ref[2 * hh + c, :, qs] + jnp.sum(ps[u], axis=0, keepdims=True)
            acc_ref[2 * hh + c, :, qs] = alpha * acc_ref[2 * hh + c, :, qs] + pv[u]
            m_ref[2 * hh + c, :, qs] = m_new[u]

    @pl.when(j < i)
    def _():
        update(False)

    @pl.when(j == i)
    def _():
        update(True)
        lam = _lam_of(lam_ref, lam_init)
        for hh in group_heads:
            o = acc_ref[2 * hh] / l_ref[2 * hh] - lam * (acc_ref[2 * hh + 1] / l_ref[2 * hh + 1])
            o = o * lax.rsqrt(jnp.mean(o * o, axis=0, keepdims=True) + SUBLN_EPS)
            o_ref[:, hh * DA_V_DIM:(hh + 1) * DA_V_DIM] = o.T * subln_ref[...] * (1.0 - lam_init)


def _alibi_slopes():
    return jnp.exp2(-ALIBI_MAX_BIAS / DA_HEADS * jnp.arange(1, DA_HEADS + 1, dtype=F32))


ATTN_Q_BLOCK = 256
ATTN_HEADS_PER_STEP = 4


def _attn_prompt(q, k, vt, lam_params, subln_w, lam_init, batch, seq):
    n = batch * seq
    tq = tk = min(512, seq)
    assert tq % ATTN_Q_BLOCK == 0 and DA_HEADS % ATTN_HEADS_PER_STEP == 0
    nq = seq // tq
    hps = ATTN_HEADS_PER_STEP
    width = hps * DA_V_DIM
    pairs = [(i, j) for i in range(nq) for j in range(i + 1)]
    qi = jnp.asarray([pr[0] for pr in pairs], jnp.int32)
    kj = jnp.asarray([pr[1] for pr in pairs], jnp.int32)
    body = functools.partial(_attn_prompt_body, lam_init=lam_init, tq=tq, tk=tk)
    grid_spec = pltpu.PrefetchScalarGridSpec(
        num_scalar_prefetch=3,
        grid=(batch, DA_HEADS // hps, len(pairs)),
        in_specs=[pl.BlockSpec((tq, width), lambda b, h, t, qi, kj, sl: (b * nq + qi[t], h)),
                  pl.BlockSpec((tk, width), lambda b, h, t, qi, kj, sl: (b * nq + kj[t], h)),
                  pl.BlockSpec((width, tk), lambda b, h, t, qi, kj, sl: (h, b * nq + kj[t])),
                  pl.BlockSpec((4, DA_HEAD_DIM), lambda b, h, t, qi, kj, sl: (0, 0)),
                  pl.BlockSpec((1, DA_V_DIM), lambda b, h, t, qi, kj, sl: (0, 0))],
        out_specs=pl.BlockSpec((tq, width), lambda b, h, t, qi, kj, sl: (b * nq + qi[t], h)),
        scratch_shapes=[pltpu.VMEM((2 * hps, 1, tq), F32), pltpu.VMEM((2 * hps, 1, tq), F32),
                        pltpu.VMEM((2 * hps, DA_V_DIM, tq), F32), pltpu.VMEM((hps, tk, tq), F32)],
    )
    return pl.pallas_call(
        body,
        grid_spec=grid_spec,
        out_shape=jax.ShapeDtypeStruct((n, DA_WIDTH), F32),
        compiler_params=_params("parallel", "parallel", "arbitrary"),
        name="attn_prompt",
    )(qi, kj, _alibi_slopes(), q, k, vt, lam_params, subln_w.reshape(1, DA_V_DIM))


def _attn_sample_body(pt_ref, slopes_ref, q_ref, kn_ref, vn_ref, lam_ref, subln_ref, ck_hbm, cv_hbm, o_ref,
                      kbuf, vbuf, sem, qbd_ref, m_ref, l_ref, acc_ref, slope_ref, bias_ref,
                      *, lam_init, layer, group, page_size, n_pages, n_new, n_seqs):
    b = pl.program_id(0)
    rows2 = 2 * n_new
    n_past = n_pages * page_size
    n_groups = n_pages // group
    n_stream = n_seqs * n_groups
    scale = DA_HEAD_DIM ** -0.5
    heads = range(DA_HEADS)
    pages = range(group)

    def group_copies(n):
        seq = n // n_groups
        first_page = (n - seq * n_groups) * group
        slot = n % DECODE_SLOTS
        copies = []
        for u in pages:
            page = pt_ref[seq, first_page + u]
            copies.append(pltpu.make_async_copy(ck_hbm.at[layer, page], kbuf.at[slot, u], sem.at[0, slot]))
            copies.append(pltpu.make_async_copy(cv_hbm.at[layer, page], vbuf.at[slot, u], sem.at[1, slot]))
        return copies

    @pl.when(b == 0)
    def _():
        for n in range(min(DECODE_LOOKAHEAD, n_stream)):
            for cp in group_copies(n):
                cp.start()

    q = q_ref[...] * scale
    lane = lax.broadcasted_iota(jnp.int32, (n_new, DA_V_DIM), 1)
    for h in heads:
        qh = q[:, h * DA_V_DIM:(h + 1) * DA_V_DIM]
        top = jnp.where(lane < DA_HEAD_DIM, qh, 0.0)
        bot = jnp.where(lane >= DA_HEAD_DIM, qh, 0.0)
        qbd_ref[h] = jnp.concatenate([top, bot], axis=0)
    m_ref[...] = jnp.full(m_ref.shape, NEG_BIG, F32)
    l_ref[...] = jnp.zeros(l_ref.shape, F32)
    acc_ref[...] = jnp.zeros(acc_ref.shape, F32)

    all_rows = DA_HEADS * rows2
    page_rows = DA_HEADS * page_size

    @pl.when(b == 0)
    def _():
        r = lax.broadcasted_iota(jnp.int32, (all_rows, page_rows), 0)
        c = lax.broadcasted_iota(jnp.int32, (all_rows, page_rows), 1)
        row_head = r // rows2
        slope = jnp.zeros((all_rows, page_rows), F32)
        for h in heads:
            slope = jnp.where(row_head == h, slopes_ref[h], slope)
        dist = (n_past + r % n_new - c // DA_HEADS).astype(F32)
        slope_ref[...] = slope
        bias_ref[...] = jnp.where(c % DA_HEADS == row_head, -slope * dist, NEG_BIG)

    def group_step(g, carry):
        n = b * n_groups + g
        slot = n % DECODE_SLOTS

        @pl.when(n + DECODE_LOOKAHEAD < n_stream)
        def _():
            for cp in group_copies(n + DECODE_LOOKAHEAD):
                cp.start()

        for cp in group_copies(n):
            cp.wait()

        q_all = qbd_ref[...].reshape(all_rows, DA_V_DIM).astype(BF16)
        slope = slope_ref[...]
        ss = []
        for u in pages:
            first_key = jnp.full((1, 1), (g * group + u) * page_size, jnp.int32).astype(F32)
            s = lax.dot_general(q_all, kbuf[slot, u].astype(BF16), NT_DIMS, preferred_element_type=F32)
            ss.append(s + (bias_ref[...] + slope * first_key))
        m_prev = m_ref[...].reshape(all_rows, 1)
        m_new = jnp.maximum(m_prev, jnp.max(functools.reduce(jnp.maximum, ss), axis=-1, keepdims=True))
        ps = [jnp.exp(s - m_new) for s in ss]
        pv = None
        for u in pages:
            d = jnp.dot(ps[u].astype(BF16), vbuf[slot, u].astype(BF16), preferred_element_type=F32)
            pv = d if pv is None else pv + d
        alpha = jnp.exp(m_prev - m_new)
        l_new = alpha * l_ref[...].reshape(all_rows, 1) + jnp.sum(functools.reduce(jnp.add, ps), axis=-1,
                                                                   keepdims=True)
        acc_new = alpha * acc_ref[...].reshape(all_rows, DA_V_DIM) + pv
        l_ref[...] = l_new.reshape(DA_HEADS, rows2, 1)
        acc_ref[...] = acc_new.reshape(DA_HEADS, rows2, DA_V_DIM)
        m_ref[...] = m_new.reshape(DA_HEADS, rows2, 1)
        return carry

    lax.fori_loop(0, n_groups, group_step, 0)

    lam = _lam_of(lam_ref, lam_init)
    qn = lax.broadcasted_iota(jnp.int32, (rows2, n_new), 0) % n_new
    kn = lax.broadcasted_iota(jnp.int32, (rows2, n_new), 1)
    dist_new = qn - kn
    for h in heads:
        hs = slice(h * DA_V_DIM, (h + 1) * DA_V_DIM)
        s = lax.dot_general(qbd_ref[h], kn_ref[:, hs], NT_DIMS, preferred_element_type=F32)
        s = s - slopes_ref[h] * dist_new.astype(F32)
        s = jnp.where(dist_new >= 0, s, NEG_BIG)
        m_prev = m_ref[h]
        m_new = jnp.maximum(m_prev, jnp.max(s, axis=-1, keepdims=True))
        alpha = jnp.exp(m_prev - m_new)
        p = jnp.exp(s - m_new)
        l = alpha * l_ref[h] + jnp.sum(p, axis=-1, keepdims=True)
        acc = alpha * acc_ref[h] + jnp.dot(p, vn_ref[:, hs], preferred_element_type=F32)
        o_ref[:, hs] = _diff_finish(acc[:n_new], l[:n_new], acc[n_new:], l[n_new:], lam,
                                    subln_ref[...], lam_init)


DECODE_PAGE_GROUP = 8
DECODE_LOOKAHEAD = 3
DECODE_SLOTS = DECODE_LOOKAHEAD + 1


def _attn_sample(q, k_new, v_new, cache_k, cache_v, page_table, lam_params, subln_w, lam_init, layer,
                 dec_batch, dec_seq):
    page_rows = cache_k.shape[2]
    page_size = page_rows // DA_HEADS
    n_pages = page_table.shape[1]
    group = math.gcd(DECODE_PAGE_GROUP, n_pages)
    body = functools.partial(_attn_sample_body, lam_init=lam_init, layer=layer, group=group,
                             page_size=page_size, n_pages=n_pages, n_new=dec_seq, n_seqs=dec_batch)
    tok = pl.BlockSpec((dec_seq, DA_QK), lambda b, pt, sl: (b, 0))
    grid_spec = pltpu.PrefetchScalarGridSpec(
        num_scalar_prefetch=2,
        grid=(dec_batch,),
        in_specs=[tok, tok, tok,
                  pl.BlockSpec((4, DA_HEAD_DIM), lambda b, pt, sl: (0, 0)),
                  pl.BlockSpec((1, DA_V_DIM), lambda b, pt, sl: (0, 0)),
                  pl.BlockSpec(memory_space=pl.ANY),
                  pl.BlockSpec(memory_space=pl.ANY)],
        out_specs=tok,
        scratch_shapes=[pltpu.VMEM((DECODE_SLOTS, group, page_rows, DA_V_DIM), F32),
                        pltpu.VMEM((DECODE_SLOTS, group, page_rows, DA_V_DIM), F32),
                        pltpu.SemaphoreType.DMA((2, DECODE_SLOTS)),
                        pltpu.VMEM((DA_HEADS, 2 * dec_seq, DA_V_DIM), F32),
                        pltpu.VMEM((DA_HEADS, 2 * dec_seq, 1), F32),
                        pltpu.VMEM((DA_HEADS, 2 * dec_seq, 1), F32),
                        pltpu.VMEM((DA_HEADS, 2 * dec_seq, DA_V_DIM), F32),
                        pltpu.VMEM((DA_HEADS * 2 * dec_seq, page_rows), F32),
                        pltpu.VMEM((DA_HEADS * 2 * dec_seq, page_rows), F32)],
    )
    return pl.pallas_call(
        body,
        grid_spec=grid_spec,
        out_shape=jax.ShapeDtypeStruct((dec_batch * dec_seq, DA_WIDTH), F32),
        compiler_params=_params("arbitrary"),
        name="attn_sample",
    )(page_table, _alibi_slopes(), q, k_new, v_new, lam_params, subln_w.reshape(1, DA_V_DIM),
      cache_k, cache_v)


FFN_CHUNK = 256


def _ffn_body(x_ref, orw_ref, oda_ref, wo_ref, g2_ref, w1_ref, w3_ref, w2_ref, fg_ref, out_ref, *, final):
    x1 = (x_ref[...]
          + jnp.dot(orw_ref[...].astype(BF16), wo_ref[:RW_WIDTH, :], preferred_element_type=F32)
          + jnp.dot(oda_ref[...].astype(BF16), wo_ref[RW_WIDTH:, :], preferred_element_type=F32))
    hf = _rms(x1, g2_ref[...], NORM_EPS).astype(BF16)
    acc = jnp.zeros_like(x1)
    for c0 in range(0, D_FF, FFN_CHUNK):
        a = jnp.dot(hf, w1_ref[:, c0:c0 + FFN_CHUNK], preferred_element_type=F32)
        b = jnp.dot(hf, w3_ref[:, c0:c0 + FFN_CHUNK], preferred_element_type=F32)
        z = (a * jax.nn.sigmoid(a) * b).astype(BF16)
        acc = acc + jnp.dot(z, w2_ref[c0:c0 + FFN_CHUNK, :], preferred_element_type=F32)
    x2 = x1 + acc
    out_ref[...] = _rms(x2, fg_ref[...], NORM_EPS) if final else x2


def _outproj_ffn(x2d, o_rw, o_da, wo, g2, w1, w3, w2, final_g, final):
    n = x2d.shape[0]
    tm = min(512, n)
    row = lambda width: pl.BlockSpec((tm, width), lambda i: (i, 0))
    full = lambda r, c: pl.BlockSpec((r, c), lambda i: (0, 0))
    return pl.pallas_call(
        functools.partial(_ffn_body, final=final),
        grid=(n // tm,),
        in_specs=[row(D_MODEL), row(RW_WIDTH), row(DA_WIDTH), full(D_MODEL, D_MODEL), full(1, D_MODEL),
                  full(D_MODEL, D_FF), full(D_MODEL, D_FF), full(D_FF, D_MODEL), full(1, D_MODEL)],
        out_specs=row(D_MODEL),
        out_shape=jax.ShapeDtypeStruct((n, D_MODEL), F32),
        compiler_params=_params("parallel"),
        name="outproj_ffn",
    )(x2d, o_rw, o_da, wo, g2.reshape(1, D_MODEL), w1, w3, w2, final_g.reshape(1, D_MODEL))


def _lambda_init(layer):
    return 0.8 - 0.6 * math.exp(-0.3 * layer)


def kernel(x_prompt, x_sample, cache_k, cache_v, state_wkv, state_shift, page_table, norm1_g, w_in, rw_mu, rw_w0, rw_w2, rw_a0, rw_a2, rw_g2, rw_k_k, rw_k_a, rw_r_k, rw_lnx_w, rw_lnx_b, da_lam_q1, da_lam_k1, da_lam_q2, da_lam_k2, da_subln_w, w_out, norm2_g, ffn_w1, ffn_w3, ffn_w2, final_g):
    depth = w_in.shape[0]
    batch, seq, _ = x_prompt.shape
    dec_batch, dec_seq, _ = x_sample.shape
    n_pool, page_size = cache_k.shape[1], cache_k.shape[2]
    ck = cache_k.reshape(depth, n_pool, page_size * DA_HEADS, 2 * DA_HEAD_DIM)
    cv = cache_v.reshape(depth, n_pool, page_size * DA_HEADS, DA_V_DIM)

    xp = x_prompt.reshape(batch * seq, D_MODEL)
    xs = x_sample.reshape(dec_batch * dec_seq, D_MODEL)
    outs = {name: [] for name in ("sp", "shp", "ks", "vs", "ss", "shs")}
    kv_prompt = []
    for l in range(depth):
        lam_init = _lambda_init(l)
        lp = dict(mu=rw_mu[l], w0=rw_w0[l], w2=rw_w2[l], a0=rw_a0[l], a2=rw_a2[l], g2=rw_g2[l],
                  k_k=rw_k_k[l], k_a=rw_k_a[l], r_k=rw_r_k[l].reshape(RW_WIDTH),
                  lnx_w=rw_lnx_w[l], lnx_b=rw_lnx_b[l])
        lam_params = jnp.stack([da_lam_q1[l], da_lam_k1[l], da_lam_q2[l], da_lam_k2[l]])
        w_in_b = w_in[l].astype(BF16)
        wo_b, w1_b, w3_b, w2_b = (w_out[l].astype(BF16), ffn_w1[l].astype(BF16),
                                  ffn_w3[l].astype(BF16), ffn_w2[l].astype(BF16))
        final = l == depth - 1

        p_rw, k_new, v_new, q_b, k_b, vt_b = _inproj_prompt(xp, norm1_g[l], w_in_b, kv_prompt if final else [])
        kv_prompt.append((k_new, v_new))
        o_rw, s_new = _rwkv_mixer(p_rw, jnp.zeros((batch, RW_PROJ), F32),
                                  jnp.zeros((batch, RW_HEADS, RW_HEAD_DIM, RW_HEAD_DIM), F32),
                                  batch, seq, lp)
        o_da = _attn_prompt(q_b, k_b, vt_b, lam_params, da_subln_w[l], lam_init, batch, seq)
        xp = _outproj_ffn(xp, o_rw, o_da, wo_b, norm2_g[l], w1_b, w3_b, w2_b, final_g, final)
        outs["sp"].append(s_new)
        outs["shp"].append(p_rw.reshape(batch, seq, RW_PROJ)[:, -1])

        p_rw, q, k, v = _inproj(xs, norm1_g[l], w_in_b)
        o_rw, s_new = _rwkv_mixer(p_rw, state_shift[l], state_wkv[l], dec_batch, dec_seq, lp)
        o_da = _attn_sample(q, k, v, ck, cv, page_table, lam_params, da_subln_w[l], lam_init, l,
                            dec_batch, dec_seq)
        xs = _outproj_ffn(xs, o_rw, o_da, wo_b, norm2_g[l], w1_b, w3_b, w2_b, final_g, final)
        outs["ks"].append(k.reshape(dec_batch, dec_seq, DA_HEADS, 2 * DA_HEAD_DIM))
        outs["vs"].append(v.reshape(dec_batch, dec_seq, DA_HEADS, DA_V_DIM))
        outs["ss"].append(s_new)
        outs["shs"].append(p_rw.reshape(dec_batch, dec_seq, RW_PROJ)[:, -1])

    kp = kv_prompt[-1][0].reshape(depth, batch, seq, DA_HEADS, 2 * DA_HEAD_DIM)
    vp = kv_prompt[-1][1].reshape(depth, batch, seq, DA_HEADS, DA_V_DIM)
    return (xp.reshape(batch, seq, D_MODEL), xs.reshape(dec_batch, dec_seq, D_MODEL),
            kp, vp, jnp.stack(outs["sp"]), jnp.stack(outs["shp"]),
            jnp.stack(outs["ks"]), jnp.stack(outs["vs"]), jnp.stack(outs["ss"]), jnp.stack(outs["shs"]))
```

```python
import functools
import math

import jax
import jax.numpy as jnp
import numpy as np
from jax import lax
from jax.experimental import pallas as pl
from jax.experimental.pallas import tpu as pltpu

F32 = jnp.float32
BF16 = jnp.bfloat16

D_MODEL = 1024
RW_HEAD_DIM = 64
RW_WIDTH = 512
RW_HEADS = RW_WIDTH // RW_HEAD_DIM
RW_DECAY_LORA = 64
RW_AAA_LORA = 64
RW_GATE_LORA = 128
RW_LN_EPS = 64e-5
RW_PROJ = 3 * RW_WIDTH + RW_DECAY_LORA + RW_AAA_LORA + RW_GATE_LORA
DA_HEAD_DIM = 64
DA_V_DIM = 2 * DA_HEAD_DIM
DA_HEADS = 4
DA_QK = DA_HEADS * 2 * DA_HEAD_DIM
DA_WIDTH = DA_HEADS * DA_V_DIM
ALIBI_MAX_BIAS = 8.0
D_FF = 2816
NORM_EPS = 1e-5
SUBLN_EPS = 1e-5
NEG_BIG = -1e30
LOG2E = 1.4426950408889634

LANES = 128
MXU_WIDTH = 256
RW_PAIRS = RW_WIDTH // LANES
RW_CHUNK = 64
RW_GROUP = 2

VMEM_LIMIT_BYTES = 56 * 1024 * 1024

NT_DIMS = (((1,), (1,)), ((), ()))
TN_DIMS = (((0,), (0,)), ((), ()))


def _params(*semantics):
    return pltpu.CompilerParams(dimension_semantics=semantics,
                                vmem_limit_bytes=VMEM_LIMIT_BYTES)


def _rms(x, g, eps):
    return x * lax.rsqrt(jnp.mean(x * x, axis=-1, keepdims=True) + eps) * g


def _mm(a, b, dims=None):
    a, b = a.astype(BF16), b.astype(BF16)
    if dims is None:
        return jnp.dot(a, b, preferred_element_type=F32)
    return lax.dot_general(a, b, dims, preferred_element_type=F32)


def _split_terms(x, terms):
    out = []
    for _ in range(terms):
        hi = x.astype(BF16)
        out.append(hi)
        x = x - hi.astype(F32)
    return out


def _inproj_body(x_ref, g_ref, w_ref, prw_ref, q_ref, k_ref, v_ref):
    h = _rms(x_ref[...], g_ref[...], NORM_EPS).astype(BF16)
    c0, c1, c2 = RW_PROJ, RW_PROJ + DA_QK, RW_PROJ + 2 * DA_QK
    prw_ref[...] = jnp.dot(h, w_ref[:, :c0], preferred_element_type=F32)
    q_ref[...] = jnp.dot(h, w_ref[:, c0:c1], preferred_element_type=F32)
    k_ref[...] = jnp.dot(h, w_ref[:, c1:c2], preferred_element_type=F32)
    v_ref[...] = jnp.dot(h, w_ref[:, c2:], preferred_element_type=F32)


def _inproj_prompt_body(x_ref, g_ref, w_ref, wvt_ref, *rest, n_prev):
    prev = rest[:2 * n_prev]
    prw_ref, k_ref, v_ref, qb_ref, kb_ref, vtb_ref = rest[2 * n_prev:]
    if n_prev:
        for li in range(n_prev):
            k_ref[li] = prev[2 * li][...]
            v_ref[li] = prev[2 * li + 1][...]
        k_ref, v_ref = k_ref.at[n_prev], v_ref.at[n_prev]
    tm = x_ref.shape[0]
    h = _rms(x_ref[...], g_ref[...], NORM_EPS).astype(BF16)
    c0, c1, c2 = RW_PROJ, RW_PROJ + DA_QK, RW_PROJ + 2 * DA_QK
    prw_ref[...] = jnp.dot(h, w_ref[:, :c0], preferred_element_type=F32)
    q = jnp.dot(h, w_ref[:, c0:c1], preferred_element_type=F32)
    qb_ref[...] = (q * (DA_HEAD_DIM ** -0.5 * LOG2E)).astype(BF16)
    k = jnp.dot(h, w_ref[:, c1:c2], preferred_element_type=F32)
    kb_ref[...] = k.astype(BF16)
    v = jnp.dot(h, w_ref[:, c2:], preferred_element_type=F32)
    for hd in range(DA_HEADS):
        head_rows = pl.ds(hd, tm, stride=DA_HEADS)
        k_ref[head_rows, :] = k[:, hd * DA_V_DIM:(hd + 1) * DA_V_DIM]
        v_ref[head_rows, :] = v[:, hd * DA_V_DIM:(hd + 1) * DA_V_DIM]
    vtb_ref[...] = lax.dot_general(wvt_ref[...], h, NT_DIMS, preferred_element_type=F32).astype(BF16)


def _inproj_prompt(x2d, g, w_bf16, kv_prev):
    n = x2d.shape[0]
    tm = min(512, n)
    in_proj = w_bf16.shape[1]
    wvt = w_bf16[:, RW_PROJ + 2 * DA_QK:].T
    n_prev = len(kv_prev)
    row = lambda width: pl.BlockSpec((tm, width), lambda i: (i, 0))
    cache_rows = pl.BlockSpec((tm * DA_HEADS, DA_V_DIM), lambda i: (i, 0))
    if n_prev:
        cache_out = pl.BlockSpec((n_prev + 1, tm * DA_HEADS, DA_V_DIM), lambda i: (0, i, 0))
        cache_shape = jax.ShapeDtypeStruct((n_prev + 1, n * DA_HEADS, DA_V_DIM), F32)
    else:
        cache_out = cache_rows
        cache_shape = jax.ShapeDtypeStruct((n * DA_HEADS, DA_V_DIM), F32)
    return pl.pallas_call(
        functools.partial(_inproj_prompt_body, n_prev=n_prev),
        grid=(n // tm,),
        in_specs=[row(D_MODEL),
                  pl.BlockSpec((1, D_MODEL), lambda i: (0, 0)),
                  pl.BlockSpec((D_MODEL, in_proj), lambda i: (0, 0)),
                  pl.BlockSpec((DA_WIDTH, D_MODEL), lambda i: (0, 0))] + [cache_rows] * (2 * n_prev),
        out_specs=[row(RW_PROJ), cache_out, cache_out, row(DA_QK), row(DA_QK),
                   pl.BlockSpec((DA_WIDTH, tm), lambda i: (0, i))],
        out_shape=[jax.ShapeDtypeStruct((n, RW_PROJ), F32), cache_shape, cache_shape,
                   jax.ShapeDtypeStruct((n, DA_QK), BF16),
                   jax.ShapeDtypeStruct((n, DA_QK), BF16),
                   jax.ShapeDtypeStruct((DA_WIDTH, n), BF16)],
        compiler_params=_params("parallel"),
        name="inproj_prompt",
    )(x2d, g.reshape(1, D_MODEL), w_bf16, wvt, *[a for kv in kv_prev for a in kv])


def _inproj(x2d, g, w_bf16):
    n = x2d.shape[0]
    tm = min(512, n)
    in_proj = w_bf16.shape[1]
    row = lambda width: pl.BlockSpec((tm, width), lambda i: (i, 0))
    return pl.pallas_call(
        _inproj_body,
        grid=(n // tm,),
        in_specs=[row(D_MODEL),
                  pl.BlockSpec((1, D_MODEL), lambda i: (0, 0)),
                  pl.BlockSpec((D_MODEL, in_proj), lambda i: (0, 0))],
        out_specs=[row(RW_PROJ), row(DA_QK), row(DA_QK), row(DA_WIDTH)],
        out_shape=[jax.ShapeDtypeStruct((n, RW_PROJ), F32),
                   jax.ShapeDtypeStruct((n, DA_QK), F32),
                   jax.ShapeDtypeStruct((n, DA_QK), F32),
                   jax.ShapeDtypeStruct((n, DA_WIDTH), F32)],
        compiler_params=_params("parallel"),
        name="inproj",
    )(x2d, g.reshape(1, D_MODEL), w_bf16)


def _softplus(z):
    return jnp.maximum(z, 0.0) + jnp.log(1.0 + jnp.exp(-jnp.abs(z)))


def _head_sums(x, ones_heads):
    width = ones_heads.shape[0]
    cols = [jnp.dot(x[:, c0:c0 + width].astype(BF16), ones_heads, preferred_element_type=F32)
            for c0 in range(0, RW_WIDTH, width)]
    return jnp.concatenate(cols, axis=-1)


def _cat0(*xs):
    return jnp.concatenate(xs, axis=0)


def _cat1(*xs):
    return jnp.concatenate(xs, axis=1)


def _rwkv_body(p_ref, prev_ref, s0_ref, mu_ref, w0_ref, w2_ref, a0_ref, a2_ref, g2_ref,
               kk_ref, ka_ref, rk_ref, lnw_ref, lnb_ref, tril_ref, ones_ref,
               o_ref, sout_ref, carry_ref, s_ref, *, tm, valid):
    i = pl.program_id(1)
    c_len = RW_CHUNK
    half = RW_HEAD_DIM
    zeros_hh = jnp.zeros((half, half), F32)

    @pl.when(i == 0)
    def _():
        carry_ref[...] = prev_ref[0]
        for q in range(RW_PAIRS):
            top = _cat1(s0_ref[0, 2 * q], zeros_hh)
            bot = _cat1(zeros_hh, s0_ref[0, 2 * q + 1])
            s_ref[q] = _cat0(top, bot)

    p = p_ref[...]
    prev_row = carry_ref[...]
    carry_ref[...] = p[valid - 1:valid, :]
    if valid < tm:
        p = _cat0(p, jnp.zeros((tm - valid, RW_PROJ), F32))
    rolled = pltpu.roll(p, 1, axis=0)
    row = lax.broadcasted_iota(jnp.int32, (tm, 1), 0)
    prev = jnp.where(row == 0, prev_row, rolled)
    xs_all = p + (prev - p) * mu_ref[...]

    nchunks = tm // c_len
    group = min(RW_GROUP, nchunks)
    block = group * c_len
    ones_heads = ones_ref[...]
    tril = tril_ref[...]
    c1, c2, c3 = RW_WIDTH, 2 * RW_WIDTH, 3 * RW_WIDTH
    c4 = c3 + RW_DECAY_LORA
    c5 = c4 + RW_AAA_LORA

    def prepare(r0):
        xs = xs_all[r0:r0 + block]
        r, k, v = xs[:, :c1], xs[:, c1:c2], xs[:, c2:c3]
        wd, ad, gd = xs[:, c3:c4], xs[:, c4:c5], xs[:, c5:]
        w = -_softplus(-(w0_ref[...] + jnp.dot(jnp.tanh(wd), w2_ref[...], preferred_element_type=F32))) - 0.5
        lw = -jnp.exp(w)
        a = jax.nn.sigmoid(a0_ref[...] + jnp.dot(ad, a2_ref[...], preferred_element_type=F32))
        g = jnp.dot(jax.nn.sigmoid(gd), g2_ref[...], preferred_element_type=F32)
        kk = k * kk_ref[...]
        kk = kk / jnp.maximum(jnp.sqrt(_head_sums(kk * kk, ones_heads)), 1e-12)
        k_h = k * (1.0 + (a - 1.0) * ka_ref[...])
        bonus = _head_sums(r * k_h * rk_ref[...], ones_heads) * v
        b = kk * a
        if valid < tm:
            live = row[r0:r0 + block] < valid
            lw = jnp.where(live, lw, 0.0)
            kk, b, k_h, r, v = (jnp.where(live, t, 0.0) for t in (kk, b, k_h, r, v))
        cum = sum(jnp.dot(tril, t, preferred_element_type=F32) for t in _split_terms(lw, 3))
        tot = _cat0(*[jnp.broadcast_to(cum[(c + 1) * c_len - 1:(c + 1) * c_len, :], (c_len, RW_WIDTH))
                      for c in range(group)])
        e_neg = jnp.exp(-cum)
        e_rest = jnp.exp(tot - cum)
        return dict(at=-kk * jnp.exp(cum - lw), bt=b * e_neg, kt=k_h * e_neg, rt=r * jnp.exp(cum),
                    bh=b * e_rest, kh=k_h * e_rest, v=v, ptot=jnp.exp(tot), bonus=bonus, g=g)

    blocks = [prepare(r0) for r0 in range(0, tm, block)]

    lane = lax.broadcasted_iota(jnp.int32, (c_len, LANES), 1)
    first = lane < half
    rr = lax.broadcasted_iota(jnp.int32, (LANES, LANES), 0)
    cc = lax.broadcasted_iota(jnp.int32, (LANES, LANES), 1)
    strict = rr > cc
    incl = rr >= cc
    eye = (rr == cc).astype(F32)
    zeros_ll = jnp.zeros((LANES, LANES), F32)
    steps = int(math.log2(c_len))
    inv_n = 1.0 / RW_HEAD_DIM

    def stack(x):
        return _cat0(jnp.where(first, x, 0.0), jnp.where(first, 0.0, x))

    s_cur = [s_ref[q] for q in range(RW_PAIRS)]
    outs = []
    for blk in blocks:
        units = [(c, q) for c in range(group) for q in range(RW_PAIRS)]
        every = range(len(units))

        def stacks(x):
            return [stack(x[c * c_len:(c + 1) * c_len, q * LANES:(q + 1) * LANES]) for c, q in units]

        la, lr, sb, sk, sv, sbh, skh = (stacks(blk[name]) for name in ("at", "rt", "bt", "kt", "v", "bh", "kh"))
        prod = [_mm(_cat0(la[u], lr[u]), _cat0(sb[u], sk[u]), NT_DIMS) for u in every]
        a_ab = [jnp.where(strict, prod[u][:LANES, :LANES], 0.0) for u in every]
        a_ak = [jnp.where(strict, prod[u][:LANES, LANES:], 0.0) for u in every]
        a_rb = [jnp.where(incl, prod[u][LANES:, :LANES], 0.0) for u in every]
        a_rk = [jnp.where(incl, prod[u][LANES:, LANES:], 0.0) for u in every]
        av = [_mm(a_ak[u], sv[u]) for u in every]
        pw = [_mm(a_ab[u], a_ab[u]) for u in every]
        inv = [eye + a_ab[u] for u in every]
        for m in range(1, steps):
            if m == steps - 1:
                res = [_mm(pw[u], inv[u]) for u in every]
                inv = [inv[u] + res[u] for u in every]
            else:
                res = [_mm(pw[u], _cat1(pw[u], inv[u])) for u in every]
                pw = [res[u][:, :LANES] for u in every]
                inv = [inv[u] + res[u][:, LANES:] for u in every]
        wmat = [_mm(inv[u], _cat1(la[u], av[u])) for u in every]
        z = [_cat0(wmat[u], _cat1(zeros_ll, sv[u])) for u in every]
        qy = [_mm(_cat1(a_rb[u], a_rk[u]), z[u]) for u in every]
        gn = [_mm(z[u], _cat0(sbh[u], skh[u]), TN_DIMS) for u in every]
        qt = [lr[u] + qy[u][:, :LANES] for u in every]
        qt = [qt[u][:c_len] + qt[u][c_len:] for u in every]
        y0 = [qy[u][:c_len, LANES:] + qy[u][c_len:, LANES:] for u in every]
        ys = []
        for c in range(group):
            us = [c * RW_PAIRS + q for q in range(RW_PAIRS)]
            y_pairs = [y0[u] + _mm(qt[u], s_cur[q], NT_DIMS) for q, u in enumerate(us)]
            upd = [_mm(s_cur[q], gn[u][:LANES]) for q, u in enumerate(us)]
            ptot = blk["ptot"][c * c_len:c * c_len + 1]
            s_cur = [s_cur[q] * ptot[:, q * LANES:(q + 1) * LANES] + upd[q] + gn[u][LANES:]
                     for q, u in enumerate(us)]
            ys.append(_cat1(*y_pairs))
        y = _cat0(*ys) if group > 1 else ys[0]
        mean = _head_sums(y, ones_heads) * inv_n
        d = y - mean
        var = _head_sums(d * d, ones_heads) * inv_n
        y = d * lax.rsqrt(var + RW_LN_EPS) * lnw_ref[...] + lnb_ref[...]
        outs.append((y + blk["bonus"]) * blk["g"])
    for q in range(RW_PAIRS):
        s_ref[q] = s_cur[q]
    out = _cat0(*outs) if len(outs) > 1 else outs[0]
    o_ref[...] = out[:valid]

    @pl.when(i == pl.num_programs(1) - 1)
    def _():
        for q in range(RW_PAIRS):
            sout_ref[0, 2 * q] = s_cur[q][:half, :half]
            sout_ref[0, 2 * q + 1] = s_cur[q][half:, half:]


def _rwkv_mixer(p_rw, prev_rows, s0, batch, seq, lp):
    n = batch * seq
    if seq % RW_CHUNK == 0:
        tm = valid = min(256, seq)
    else:
        assert seq < RW_CHUNK and seq % 8 == 0
        tm, valid = RW_CHUNK, seq
    nt = seq // valid
    block = min(RW_GROUP * RW_CHUNK, tm)
    idx = np.arange(block)
    tril = (((idx[:, None] // RW_CHUNK) == (idx[None, :] // RW_CHUNK)) & (idx[None, :] <= idx[:, None]))
    lane = np.arange(MXU_WIDTH) // RW_HEAD_DIM
    ones_heads = lane[:, None] == lane[None, :]
    vec = lambda width: pl.BlockSpec((1, width), lambda b, i: (0, 0))
    full = lambda r, c: pl.BlockSpec((r, c), lambda b, i: (0, 0))
    state = pl.BlockSpec((1, RW_HEADS, RW_HEAD_DIM, RW_HEAD_DIM), lambda b, i: (b, 0, 0, 0))
    return pl.pallas_call(
        functools.partial(_rwkv_body, tm=tm, valid=valid),
        grid=(batch, nt),
        in_specs=[pl.BlockSpec((valid, RW_PROJ), lambda b, i: (b * nt + i, 0)),
                  pl.BlockSpec((1, 1, RW_PROJ), lambda b, i: (b, 0, 0)),
                  state,
                  vec(RW_PROJ), vec(RW_WIDTH), full(RW_DECAY_LORA, RW_WIDTH),
                  vec(RW_WIDTH), full(RW_AAA_LORA, RW_WIDTH), full(RW_GATE_LORA, RW_WIDTH),
                  vec(RW_WIDTH), vec(RW_WIDTH), vec(RW_WIDTH), vec(RW_WIDTH), vec(RW_WIDTH),
                  full(block, block), full(MXU_WIDTH, MXU_WIDTH)],
        out_specs=[pl.BlockSpec((valid, RW_WIDTH), lambda b, i: (b * nt + i, 0)), state],
        out_shape=[jax.ShapeDtypeStruct((n, RW_WIDTH), F32),
                   jax.ShapeDtypeStruct((batch, RW_HEADS, RW_HEAD_DIM, RW_HEAD_DIM), F32)],
        scratch_shapes=[pltpu.VMEM((1, RW_PROJ), F32),
                        pltpu.VMEM((RW_PAIRS, LANES, LANES), F32)],
        compiler_params=_params("parallel", "arbitrary"),
        name="rwkv",
    )(p_rw, prev_rows.reshape(batch, 1, RW_PROJ), s0,
      lp["mu"].reshape(1, RW_PROJ), lp["w0"].reshape(1, RW_WIDTH), lp["w2"],
      lp["a0"].reshape(1, RW_WIDTH), lp["a2"], lp["g2"],
      lp["k_k"].reshape(1, RW_WIDTH), lp["k_a"].reshape(1, RW_WIDTH), lp["r_k"].reshape(1, RW_WIDTH),
      lp["lnx_w"].reshape(1, RW_WIDTH), lp["lnx_b"].reshape(1, RW_WIDTH),
      jnp.asarray(tril, BF16), jnp.asarray(ones_heads, BF16))


def _lam_of(lam_ref, lam_init):
    lq1, lk1, lq2, lk2 = lam_ref[0:1, :], lam_ref[1:2, :], lam_ref[2:3, :], lam_ref[3:4, :]
    return (jnp.exp(jnp.sum(lq1 * lk1, axis=-1, keepdims=True))
            - jnp.exp(jnp.sum(lq2 * lk2, axis=-1, keepdims=True)) + lam_init)


def _diff_finish(acc1, l1, acc2, l2, lam, subln_w, lam_init):
    o = acc1 / l1 - lam * (acc2 / l2)
    o = o * lax.rsqrt(jnp.mean(o * o, axis=-1, keepdims=True) + SUBLN_EPS) * subln_w
    return o * (1.0 - lam_init)


def _attn_prompt_body(qi_ref, kj_ref, slopes_ref, q_ref, k_ref, vt_ref, lam_ref, subln_ref, o_ref,
                      m_ref, l_ref, acc_ref, bias_ref, *, lam_init, tq, tk):
    hg = pl.program_id(1)
    t = pl.program_id(2)
    i = qi_ref[t]
    j = kj_ref[t]
    group_heads = range(ATTN_HEADS_PER_STEP)
    neg_slope = [-slopes_ref[hg * ATTN_HEADS_PER_STEP + hh] * LOG2E for hh in group_heads]

    @pl.when(t == 0)
    def _():
        kpos = lax.broadcasted_iota(jnp.int32, (tk, tq), 0)
        qpos = lax.broadcasted_iota(jnp.int32, (tk, tq), 1)
        dist = (qpos - kpos).astype(F32)
        for hh in group_heads:
            bias_ref[hh] = dist * neg_slope[hh]

    @pl.when(j == 0)
    def _():
        m_ref[...] = jnp.full(m_ref.shape, NEG_BIG, F32)
        l_ref[...] = jnp.zeros(l_ref.shape, F32)
        acc_ref[...] = jnp.zeros(acc_ref.shape, F32)

    tile_dist = jnp.full((1, 1), i * tq - j * tk, jnp.int32).astype(F32)
    off = [tile_dist * neg_slope[hh] for hh in group_heads]

    def update(on_diagonal):
        q = q_ref[...]
        k = k_ref[...]
        vt = vt_ref[...]
        units = [(hh, c, q0) for hh in group_heads for c in range(2) for q0 in range(0, tq, ATTN_Q_BLOCK)]
        every = range(len(units))
        ss = []
        for hh, c, q0 in units:
            sl = slice(hh * DA_V_DIM + c * DA_HEAD_DIM, hh * DA_V_DIM + (c + 1) * DA_HEAD_DIM)
            qs = slice(q0, q0 + ATTN_Q_BLOCK)
            s = lax.dot_general(k[:, sl], q[qs, sl], NT_DIMS, preferred_element_type=F32) + bias_ref[hh, :, qs]
            if on_diagonal:
                kpos = lax.broadcasted_iota(jnp.int32, (tk, ATTN_Q_BLOCK), 0)
                qpos = lax.broadcasted_iota(jnp.int32, (tk, ATTN_Q_BLOCK), 1) + q0
                s = jnp.where(qpos >= kpos, s, NEG_BIG)
            ss.append(s)
        m_prev = [m_ref[2 * hh + c, :, q0:q0 + ATTN_Q_BLOCK] for hh, c, q0 in units]
        m_new = [jnp.maximum(m_prev[u], jnp.max(ss[u], axis=0, keepdims=True) + off[units[u][0]]) for u in every]
        ps = [jnp.exp2(ss[u] - (m_new[u] - off[units[u][0]])) for u in every]
        pv = [jnp.dot(vt[units[u][0] * DA_V_DIM:(units[u][0] + 1) * DA_V_DIM], ps[u].astype(BF16),
                      preferred_element_type=F32) for u in every]
        for u, (hh, c, q0) in enumerate(units):
            qs = slice(q0, q0 + ATTN_Q_BLOCK)
            alpha = jnp.exp2(m_prev[u] - m_new[u])
            l_ref[2 * hh + c, :, qs] = alpha * l_ref[2 * hh + c, :, qs] + jnp.sum(ps[u], axis=0, keepdims=True)
            acc_ref[2 * hh + c, :, qs] = alpha * acc_ref[2 * hh + c, :, qs] + pv[u]
            m_ref[2 * hh + c, :, qs] = m_new[u]

    @pl.when(j < i)
    def _():
        update(False)

    @pl.when(j == i)
    def _():
        update(True)
        lam = _lam_of(lam_ref, lam_init)
        for hh in group_heads:
            o = acc_ref[2 * hh] / l_ref[2 * hh] - lam * (acc_ref[2 * hh + 1] / l_ref[2 * hh + 1])
            o = o * lax.rsqrt(jnp.mean(o * o, axis=0, keepdims=True) + SUBLN_EPS)
            o_ref[:, hh * DA_V_DIM:(hh + 1) * DA_V_DIM] = o.T * subln_ref[...] * (1.0 - lam_init)


def _alibi_slopes():
    return jnp.exp2(-ALIBI_MAX_BIAS / DA_HEADS * jnp.arange(1, DA_HEADS + 1, dtype=F32))


ATTN_Q_BLOCK = 256
ATTN_HEADS_PER_STEP = 4


def _attn_prompt(q, k, vt, lam_params, subln_w, lam_init, batch, seq):
    n = batch * seq
    tq = tk = min(512, seq)
    assert tq % ATTN_Q_BLOCK == 0 and DA_HEADS % ATTN_HEADS_PER_STEP == 0
    nq = seq // tq
    hps = ATTN_HEADS_PER_STEP
    width = hps * DA_V_DIM
    pairs = [(i, j) for i in range(nq) for j in range(i + 1)]
    qi = jnp.asarray([pr[0] for pr in pairs], jnp.int32)
    kj = jnp.asarray([pr[1] for pr in pairs], jnp.int32)
    body = functools.partial(_attn_prompt_body, lam_init=lam_init, tq=tq, tk=tk)
    grid_spec = pltpu.PrefetchScalarGridSpec(
        num_scalar_prefetch=3,
        grid=(batch, DA_HEADS // hps, len(pairs)),
        in_specs=[pl.BlockSpec((tq, width), lambda b, h, t, qi, kj, sl: (b * nq + qi[t], h)),
                  pl.BlockSpec((tk, width), lambda b, h, t, qi, kj, sl: (b * nq + kj[t], h)),
                  pl.BlockSpec((width, tk), lambda b, h, t, qi, kj, sl: (h, b * nq + kj[t])),
                  pl.BlockSpec((4, DA_HEAD_DIM), lambda b, h, t, qi, kj, sl: (0, 0)),
                  pl.BlockSpec((1, DA_V_DIM), lambda b, h, t, qi, kj, sl: (0, 0))],
        out_specs=pl.BlockSpec((tq, width), lambda b, h, t, qi, kj, sl: (b * nq + qi[t], h)),
        scratch_shapes=[pltpu.VMEM((2 * hps, 1, tq), F32), pltpu.VMEM((2 * hps, 1, tq), F32),
                        pltpu.VMEM((2 * hps, DA_V_DIM, tq), F32), pltpu.VMEM((hps, tk, tq), F32)],
    )
    return pl.pallas_call(
        body,
        grid_spec=grid_spec,
        out_shape=jax.ShapeDtypeStruct((n, DA_WIDTH), F32),
        compiler_params=_params("parallel", "parallel", "arbitrary"),
        name="attn_prompt",
    )(qi, kj, _alibi_slopes(), q, k, vt, lam_params, subln_w.reshape(1, DA_V_DIM))


def _attn_sample_body(pt_ref, slopes_ref, q_ref, kn_ref, vn_ref, lam_ref, subln_ref, ck_hbm, cv_hbm, o_ref,
                      kbuf, vbuf, sem, qbd_ref, m_ref, l_ref, acc_ref, slope_ref, bias_ref,
                      *, lam_init, layer, group, page_size, n_pages, n_new, n_seqs):
    b = pl.program_id(0)
    rows2 = 2 * n_new
    n_past = n_pages * page_size
    n_groups = n_pages // group
    n_stream = n_seqs * n_groups
    scale = DA_HEAD_DIM ** -0.5
    heads = range(DA_HEADS)
    pages = range(group)

    def group_copies(n):
        seq = n // n_groups
        first_page = (n - seq * n_groups) * group
        slot = n % DECODE_SLOTS
        copies = []
        for u in pages:
            page = pt_ref[seq, first_page + u]
            copies.append(pltpu.make_async_copy(ck_hbm.at[layer, page], kbuf.at[slot, u], sem.at[0, slot]))
            copies.append(pltpu.make_async_copy(cv_hbm.at[layer, page], vbuf.at[slot, u], sem.at[1, slot]))
        return copies

    @pl.when(b == 0)
    def _():
        for n in range(min(DECODE_LOOKAHEAD, n_stream)):
            for idx, cp in enumerate(group_copies(n)):
                cp.start(priority=idx % 2)

    q = q_ref[...] * scale
    lane = lax.broadcasted_iota(jnp.int32, (n_new, DA_V_DIM), 1)
    for h in heads:
        qh = q[:, h * DA_V_DIM:(h + 1) * DA_V_DIM]
        top = jnp.where(lane < DA_HEAD_DIM, qh, 0.0)
        bot = jnp.where(lane >= DA_HEAD_DIM, qh, 0.0)
        qbd_ref[h] = jnp.concatenate([top, bot], axis=0)
    m_ref[...] = jnp.full(m_ref.shape, NEG_BIG, F32)
    l_ref[...] = jnp.zeros(l_ref.shape, F32)
    acc_ref[...] = jnp.zeros(acc_ref.shape, F32)

    all_rows = DA_HEADS * rows2
    page_rows = DA_HEADS * page_size

    @pl.when(b == 0)
    def _():
        r = lax.broadcasted_iota(jnp.int32, (all_rows, page_rows), 0)
        c = lax.broadcasted_iota(jnp.int32, (all_rows, page_rows), 1)
        row_head = r // rows2
        slope = jnp.zeros((all_rows, page_rows), F32)
        for h in heads:
            slope = jnp.where(row_head == h, slopes_ref[h], slope)
        dist = (n_past + r % n_new - c // DA_HEADS).astype(F32)
        slope_ref[...] = slope
        bias_ref[...] = jnp.where(c % DA_HEADS == row_head, -slope * dist, NEG_BIG)

    def group_step(g, carry):
        n = b * n_groups + g
        slot = n % DECODE_SLOTS

        @pl.when(n + DECODE_LOOKAHEAD < n_stream)
        def _():
            for idx, cp in enumerate(group_copies(n + DECODE_LOOKAHEAD)):
                cp.start(priority=idx % 2)

        for cp in group_copies(n):
            cp.wait()

        q_all = qbd_ref[...].reshape(all_rows, DA_V_DIM).astype(BF16)
        slope = slope_ref[...]
        ss = []
        for u in pages:
            first_key = jnp.full((1, 1), (g * group + u) * page_size, jnp.int32).astype(F32)
            s = lax.dot_general(q_all, kbuf[slot, u].astype(BF16), NT_DIMS, preferred_element_type=F32)
            ss.append(s + (bias_ref[...] + slope * first_key))
        m_prev = m_ref[...].reshape(all_rows, 1)
        m_new = jnp.maximum(m_prev, jnp.max(functools.reduce(jnp.maximum, ss), axis=-1, keepdims=True))
        ps = [jnp.exp(s - m_new) for s in ss]
        pv = None
        for u in pages:
            d = jnp.dot(ps[u].astype(BF16), vbuf[slot, u].astype(BF16), preferred_element_type=F32)
            pv = d if pv is None else pv + d
        alpha = jnp.exp(m_prev - m_new)
        l_new = alpha * l_ref[...].reshape(all_rows, 1) + jnp.sum(functools.reduce(jnp.add, ps), axis=-1,
                                                                   keepdims=True)
        acc_new = alpha * acc_ref[...].reshape(all_rows, DA_V_DIM) + pv
        l_ref[...] = l_new.reshape(DA_HEADS, rows2, 1)
        acc_ref[...] = acc_new.reshape(DA_HEADS, rows2, DA_V_DIM)
        m_ref[...] = m_new.reshape(DA_HEADS, rows2, 1)
        return carry

    lax.fori_loop(0, n_groups, group_step, 0)

    lam = _lam_of(lam_ref, lam_init)
    qn = lax.broadcasted_iota(jnp.int32, (rows2, n_new), 0) % n_new
    kn = lax.broadcasted_iota(jnp.int32, (rows2, n_new), 1)
    dist_new = qn - kn
    for h in heads:
        hs = slice(h * DA_V_DIM, (h + 1) * DA_V_DIM)
        s = lax.dot_general(qbd_ref[h], kn_ref[:, hs], NT_DIMS, preferred_element_type=F32)
        s = s - slopes_ref[h] * dist_new.astype(F32)
        s = jnp.where(dist_new >= 0, s, NEG_BIG)
        m_prev = m_ref[h]
        m_new = jnp.maximum(m_prev, jnp.max(s, axis=-1, keepdims=True))
        alpha = jnp.exp(m_prev - m_new)
        p = jnp.exp(s - m_new)
        l = alpha * l_ref[h] + jnp.sum(p, axis=-1, keepdims=True)
        acc = alpha * acc_ref[h] + jnp.dot(p, vn_ref[:, hs], preferred_element_type=F32)
        o_ref[:, hs] = _diff_finish(acc[:n_new], l[:n_new], acc[n_new:], l[n_new:], lam,
                                    subln_ref[...], lam_init)


DECODE_PAGE_GROUP = 8
DECODE_LOOKAHEAD = 3
DECODE_SLOTS = DECODE_LOOKAHEAD + 1


def _attn_sample(q, k_new, v_new, cache_k, cache_v, page_table, lam_params, subln_w, lam_init, layer,
                 dec_batch, dec_seq):
    page_rows = cache_k.shape[2]
    page_size = page_rows // DA_HEADS
    n_pages = page_table.shape[1]
    group = math.gcd(DECODE_PAGE_GROUP, n_pages)
    body = functools.partial(_attn_sample_body, lam_init=lam_init, layer=layer, group=group,
                             page_size=page_size, n_pages=n_pages, n_new=dec_seq, n_seqs=dec_batch)
    tok = pl.BlockSpec((dec_seq, DA_QK), lambda b, pt, sl: (b, 0))
    grid_spec = pltpu.PrefetchScalarGridSpec(
        num_scalar_prefetch=2,
        grid=(dec_batch,),
        in_specs=[tok, tok, tok,
                  pl.BlockSpec((4, DA_HEAD_DIM), lambda b, pt, sl: (0, 0)),
                  pl.BlockSpec((1, DA_V_DIM), lambda b, pt, sl: (0, 0)),
                  pl.BlockSpec(memory_space=pl.ANY),
                  pl.BlockSpec(memory_space=pl.ANY)],
        out_specs=tok,
        scratch_shapes=[pltpu.VMEM((DECODE_SLOTS, group, page_rows, DA_V_DIM), F32),
                        pltpu.VMEM((DECODE_SLOTS, group, page_rows, DA_V_DIM), F32),
                        pltpu.SemaphoreType.DMA((2, DECODE_SLOTS)),
                        pltpu.VMEM((DA_HEADS, 2 * dec_seq, DA_V_DIM), F32),
                        pltpu.VMEM((DA_HEADS, 2 * dec_seq, 1), F32),
                        pltpu.VMEM((DA_HEADS, 2 * dec_seq, 1), F32),
                        pltpu.VMEM((DA_HEADS, 2 * dec_seq, DA_V_DIM), F32),
                        pltpu.VMEM((DA_HEADS * 2 * dec_seq, page_rows), F32),
                        pltpu.VMEM((DA_HEADS * 2 * dec_seq, page_rows), F32)],
    )
    return pl.pallas_call(
        body,
        grid_spec=grid_spec,
        out_shape=jax.ShapeDtypeStruct((dec_batch * dec_seq, DA_WIDTH), F32),
        compiler_params=_params("arbitrary"),
        name="attn_sample",
    )(page_table, _alibi_slopes(), q, k_new, v_new, lam_params, subln_w.reshape(1, DA_V_DIM),
      cache_k, cache_v)


FFN_CHUNK = 256


def _ffn_body(x_ref, orw_ref, oda_ref, wo_ref, g2_ref, w1_ref, w3_ref, w2_ref, fg_ref, out_ref, *, final):
    x1 = (x_ref[...]
          + jnp.dot(orw_ref[...].astype(BF16), wo_ref[:RW_WIDTH, :], preferred_element_type=F32)
          + jnp.dot(oda_ref[...].astype(BF16), wo_ref[RW_WIDTH:, :], preferred_element_type=F32))
    hf = _rms(x1, g2_ref[...], NORM_EPS).astype(BF16)
    acc = jnp.zeros_like(x1)
    for c0 in range(0, D_FF, FFN_CHUNK):
        a = jnp.dot(hf, w1_ref[:, c0:c0 + FFN_CHUNK], preferred_element_type=F32)
        b = jnp.dot(hf, w3_ref[:, c0:c0 + FFN_CHUNK], preferred_element_type=F32)
        z = (a * jax.nn.sigmoid(a) * b).astype(BF16)
        acc = acc + jnp.dot(z, w2_ref[c0:c0 + FFN_CHUNK, :], preferred_element_type=F32)
    x2 = x1 + acc
    out_ref[...] = _rms(x2, fg_ref[...], NORM_EPS) if final else x2


def _outproj_ffn(x2d, o_rw, o_da, wo, g2, w1, w3, w2, final_g, final):
    n = x2d.shape[0]
    tm = min(512, n)
    row = lambda width: pl.BlockSpec((tm, width), lambda i: (i, 0))
    full = lambda r, c: pl.BlockSpec((r, c), lambda i: (0, 0))
    return pl.pallas_call(
        functools.partial(_ffn_body, final=final),
        grid=(n // tm,),
        in_specs=[row(D_MODEL), row(RW_WIDTH), row(DA_WIDTH), full(D_MODEL, D_MODEL), full(1, D_MODEL),
                  full(D_MODEL, D_FF), full(D_MODEL, D_FF), full(D_FF, D_MODEL), full(1, D_MODEL)],
        out_specs=row(D_MODEL),
        out_shape=jax.ShapeDtypeStruct((n, D_MODEL), F32),
        compiler_params=_params("parallel"),
        name="outproj_ffn",
    )(x2d, o_rw, o_da, wo, g2.reshape(1, D_MODEL), w1, w3, w2, final_g.reshape(1, D_MODEL))


def _lambda_init(layer):
    return 0.8 - 0.6 * math.exp(-0.3 * layer)


def kernel(x_prompt, x_sample, cache_k, cache_v, state_wkv, state_shift, page_table, norm1_g, w_in, rw_mu, rw_w0, rw_w2, rw_a0, rw_a2, rw_g2, rw_k_k, rw_k_a, rw_r_k, rw_lnx_w, rw_lnx_b, da_lam_q1, da_lam_k1, da_lam_q2, da_lam_k2, da_subln_w, w_out, norm2_g, ffn_w1, ffn_w3, ffn_w2, final_g):
    depth = w_in.shape[0]
    batch, seq, _ = x_prompt.shape
    dec_batch, dec_seq, _ = x_sample.shape
    n_pool, page_size = cache_k.shape[1], cache_k.shape[2]
    ck = cache_k.reshape(depth, n_pool, page_size * DA_HEADS, 2 * DA_HEAD_DIM)
    cv = cache_v.reshape(depth, n_pool, page_size * DA_HEADS, DA_V_DIM)

    xp = x_prompt.reshape(batch * seq, D_MODEL)
    xs = x_sample.reshape(dec_batch * dec_seq, D_MODEL)
    outs = {name: [] for name in ("sp", "shp", "ks", "vs", "ss", "shs")}
    kv_prompt = []
    for l in range(depth):
        lam_init = _lambda_init(l)
        lp = dict(mu=rw_mu[l], w0=rw_w0[l], w2=rw_w2[l], a0=rw_a0[l], a2=rw_a2[l], g2=rw_g2[l],
                  k_k=rw_k_k[l], k_a=rw_k_a[l], r_k=rw_r_k[l].reshape(RW_WIDTH),
                  lnx_w=rw_lnx_w[l], lnx_b=rw_lnx_b[l])
        lam_params = jnp.stack([da_lam_q1[l], da_lam_k1[l], da_lam_q2[l], da_lam_k2[l]])
        w_in_b = w_in[l].astype(BF16)
        wo_b, w1_b, w3_b, w2_b = (w_out[l].astype(BF16), ffn_w1[l].astype(BF16),
                                  ffn_w3[l].astype(BF16), ffn_w2[l].astype(BF16))
        final = l == depth - 1

        p_rw, k_new, v_new, q_b, k_b, vt_b = _inproj_prompt(xp, norm1_g[l], w_in_b, kv_prompt if final else [])
        kv_prompt.append((k_new, v_new))
        o_rw, s_new = _rwkv_mixer(p_rw, jnp.zeros((batch, RW_PROJ), F32),
                                  jnp.zeros((batch, RW_HEADS, RW_HEAD_DIM, RW_HEAD_DIM), F32),
                                  batch, seq, lp)
        o_da = _attn_prompt(q_b, k_b, vt_b, lam_params, da_subln_w[l], lam_init, batch, seq)
        xp = _outproj_ffn(xp, o_rw, o_da, wo_b, norm2_g[l], w1_b, w3_b, w2_b, final_g, final)
        outs["sp"].append(s_new)
        outs["shp"].append(p_rw.reshape(batch, seq, RW_PROJ)[:, -1])

        p_rw, q, k, v = _inproj(xs, norm1_g[l], w_in_b)
        o_rw, s_new = _rwkv_mixer(p_rw, state_shift[l], state_wkv[l], dec_batch, dec_seq, lp)
        o_da = _attn_sample(q, k, v, ck, cv, page_table, lam_params, da_subln_w[l], lam_init, l,
                            dec_batch, dec_seq)
        xs = _outproj_ffn(xs, o_rw, o_da, wo_b, norm2_g[l], w1_b, w3_b, w2_b, final_g, final)
        outs["ks"].append(k.reshape(dec_batch, dec_seq, DA_HEADS, 2 * DA_HEAD_DIM))
        outs["vs"].append(v.reshape(dec_batch, dec_seq, DA_HEADS, DA_V_DIM))
        outs["ss"].append(s_new)
        outs["shs"].append(p_rw.reshape(dec_batch, dec_seq, RW_PROJ)[:, -1])

    kp = kv_prompt[-1][0].reshape(depth, batch, seq, DA_HEADS, 2 * DA_HEAD_DIM)
    vp = kv_prompt[-1][1].reshape(depth, batch, seq, DA_HEADS, DA_V_DIM)
    return (xp.reshape(batch, seq, D_MODEL), xs.reshape(dec_batch, dec_seq, D_MODEL),
            kp, vp, jnp.stack(outs["sp"]), jnp.stack(outs["shp"]),
            jnp.stack(outs["ks"]), jnp.stack(outs["vs"]), jnp.stack(outs["ss"]), jnp.stack(outs["shs"]))
```
